```python
import jax, jax.numpy as jnp
from jax import lax
import numpy as np

D_MODEL = 4096
BATCH = 2
SEQ = 8192
DEPTH = 2

HEAD_DIM = 128
CONV_CH = 1024
CONV_WIDTH = 31
NSA_HEADS = 12
NSA_KV_HEADS = 4
SWA_HEADS = 12
SWA_KV_HEADS = 2
D_MIX = CONV_CH + (NSA_HEADS + SWA_HEADS) * HEAD_DIM
CMP_BLOCK = 32
CMP_STRIDE = 16
SEL_BLOCK = 64
SEL_TOPK = 16
NSA_WINDOW = 512
NSA_Q_CHUNK = 32
SWA_WINDOW = 128
SWA_BLOCK = 128
ROPE_THETA = 10000.0
D_FF = 11008
LN_EPS = 1e-5
DEEPNORM_ALPHA = (2 * DEPTH) ** 0.25
DEEPNORM_BETA = (8 * DEPTH) ** -0.25
NEG_INF = -1e30
FORCE_SCORE = 1e4
NSA_KV = NSA_KV_HEADS * HEAD_DIM
SWA_KV = SWA_KV_HEADS * HEAD_DIM
IN_SPLITS = (CONV_CH, CONV_CH, NSA_HEADS * HEAD_DIM, NSA_KV, NSA_KV, NSA_KV, NSA_KV, NSA_KV, NSA_KV, NSA_HEADS * 3, SWA_HEADS * HEAD_DIM, SWA_KV, SWA_KV)
IN_VALUE = (True, False, False, False, True, False, True, False, True, False, False, False, True)
D_IN = sum(IN_SPLITS)

kernel_name = "hybrid_conformerconv_nsa_swasink_macaron_deepnorm"


def _layernorm(x, g, b):
    xf = x.astype(jnp.float32)
    mu = jnp.mean(xf, axis=-1, keepdims=True)
    var = jnp.mean(jnp.square(xf - mu), axis=-1, keepdims=True)
    return ((xf - mu) * lax.rsqrt(var + LN_EPS) * g + b).astype(x.dtype)


def _swiglu(x, w_gu, w_down):
    gate, up = jnp.split(x @ w_gu, 2, axis=-1)
    return (jax.nn.silu(gate) * up) @ w_down


def _rope_tables(positions):
    inv = jnp.power(ROPE_THETA, -jnp.arange(0, HEAD_DIM, 2, dtype=jnp.float32) / HEAD_DIM)
    ang = positions.astype(jnp.float32)[..., None] * inv
    return jnp.cos(ang)[:, :, None, :], jnp.sin(ang)[:, :, None, :]


def _rope(x, cos, sin):
    x1, x2 = jnp.split(x.astype(jnp.float32), 2, axis=-1)
    return jnp.concatenate([x1 * cos - x2 * sin, x2 * cos + x1 * sin], axis=-1).astype(x.dtype)


def _conv_module(val, gate, dw_w, dw_b, ln_g, ln_b):
    u = val * jax.nn.sigmoid(gate)
    y = lax.conv_general_dilated(u, dw_w[:, None, :], window_strides=(1,), padding=[(CONV_WIDTH - 1, 0)],
                                 dimension_numbers=('NWC', 'WIO', 'NWC'), feature_group_count=CONV_CH) + dw_b
    return jax.nn.silu(_layernorm(y, ln_g, ln_b))


def _compress(t, pos_emb, w1, w2):
    B, S, H, Dh = t.shape
    n_cmp = (S - CMP_BLOCK) // CMP_STRIDE + 1
    idx = np.arange(n_cmp)[:, None] * CMP_STRIDE + np.arange(CMP_BLOCK)[None, :]
    blocks = t[:, idx] + pos_emb[:, None, :]
    blocks = jnp.moveaxis(blocks, 2, 3).reshape(B, n_cmp, H, CMP_BLOCK * Dh)
    return jax.nn.silu(blocks @ w1) @ w2


def _cmp_to_sel(n_cmp, n_sel):
    cs = np.arange(n_cmp) * CMP_STRIDE
    ss = np.arange(n_sel) * SEL_BLOCK
    return ((cs[:, None] < ss[None, :] + SEL_BLOCK) & (cs[:, None] + CMP_BLOCK > ss[None, :])).astype(np.float32)


def _nsa(q, k_cmp, v_cmp, k_slc, v_slc, k_win, v_win, gates):
    B, S = q.shape[:2]
    G = NSA_HEADS // NSA_KV_HEADS
    scale = HEAD_DIM ** -0.5
    n_cmp = k_cmp.shape[1]
    n_sel = S // SEL_BLOCK
    top_k = min(SEL_TOPK, n_sel)
    cmp_end = jnp.arange(n_cmp) * CMP_STRIDE + (CMP_BLOCK - 1)
    overlap = jnp.asarray(_cmp_to_sel(n_cmp, n_sel))
    kbt = k_slc.reshape(B, n_sel, SEL_BLOCK, NSA_KV_HEADS, HEAD_DIM).transpose(0, 3, 1, 2, 4)
    vbt = v_slc.reshape(B, n_sel, SEL_BLOCK, NSA_KV_HEADS, HEAD_DIM).transpose(0, 3, 1, 2, 4)
    pad = ((0, 0), (NSA_WINDOW, 0), (0, 0), (0, 0))
    kw_pad = jnp.pad(k_win, pad)
    vw_pad = jnp.pad(v_win, pad)
    qg = q.reshape(B, S, NSA_KV_HEADS, G, HEAD_DIM)
    gg = gates.reshape(B, S, NSA_KV_HEADS, G, 3)
    b_ix = jnp.arange(B)[:, None, None, None]
    h_ix = jnp.arange(NSA_KV_HEADS)[None, None, :, None]
    blk = jnp.arange(n_sel)

    def chunk(c):
        start = c * NSA_Q_CHUNK
        qc = lax.dynamic_slice_in_dim(qg, start, NSA_Q_CHUNK, axis=1)
        gc = lax.dynamic_slice_in_dim(gg, start, NSA_Q_CHUNK, axis=1)
        t_pos = start + jnp.arange(NSA_Q_CHUNK)
        s_c = jnp.einsum('bthgd,bnhd->bthgn', qc, k_cmp, preferred_element_type=jnp.float32) * scale
        m_c = (cmp_end[None, :] <= t_pos[:, None])[None, :, None, None, :]
        p_c = jnp.where(m_c, jax.nn.softmax(jnp.where(m_c, s_c, NEG_INF), axis=-1), 0.0)
        o_c = jnp.einsum('bthgn,bnhd->bthgd', p_c.astype(v_cmp.dtype), v_cmp)
        imp = jnp.einsum('bthgn,nj->bthj', p_c, overlap)
        t_blk = t_pos // SEL_BLOCK
        forced = (blk[None, :] == 0) | (blk[None, :] == t_blk[:, None]) | (blk[None, :] == t_blk[:, None] - 1)
        causal_blk = blk[None, :] <= t_blk[:, None]
        imp = jnp.where(forced[None, :, None, :], FORCE_SCORE, imp)
        imp = jnp.where(causal_blk[None, :, None, :], imp, NEG_INF)
        _, top_idx = lax.top_k(imp, top_k)
        sel_ok = top_idx <= t_blk[None, :, None, None]
        ks = kbt[b_ix, h_ix, top_idx]
        vs = vbt[b_ix, h_ix, top_idx]
        s_s = jnp.einsum('bthgd,bthkjd->bthgkj', qc, ks, preferred_element_type=jnp.float32) * scale
        key_pos = top_idx[..., None] * SEL_BLOCK + jnp.arange(SEL_BLOCK)
        m_s = (sel_ok[..., None] & (key_pos <= t_pos[None, :, None, None, None]))
        m_s = m_s.reshape(B, NSA_Q_CHUNK, NSA_KV_HEADS, 1, top_k * SEL_BLOCK)
        s_s = s_s.reshape(B, NSA_Q_CHUNK, NSA_KV_HEADS, G, top_k * SEL_BLOCK)
        p_s = jax.nn.softmax(jnp.where(m_s, s_s, NEG_INF), axis=-1).reshape(B, NSA_Q_CHUNK, NSA_KV_HEADS, G, top_k, SEL_BLOCK)
        o_s = jnp.einsum('bthgkj,bthkjd->bthgd', p_s.astype(vs.dtype), vs)
        kwc = lax.dynamic_slice_in_dim(kw_pad, start, NSA_WINDOW + NSA_Q_CHUNK, axis=1)
        vwc = lax.dynamic_slice_in_dim(vw_pad, start, NSA_WINDOW + NSA_Q_CHUNK, axis=1)
        kpos = start - NSA_WINDOW + jnp.arange(NSA_WINDOW + NSA_Q_CHUNK)
        m_w = (kpos[None, :] <= t_pos[:, None]) & (kpos[None, :] > t_pos[:, None] - NSA_WINDOW) & (kpos[None, :] >= 0)
        s_w = jnp.einsum('bthgd,bjhd->bthgj', qc, kwc, preferred_element_type=jnp.float32) * scale
        p_w = jax.nn.softmax(jnp.where(m_w[None, :, None, None, :], s_w, NEG_INF), axis=-1)
        o_w = jnp.einsum('bthgj,bjhd->bthgd', p_w.astype(vwc.dtype), vwc)
        o = gc[..., 0:1] * o_c + gc[..., 1:2] * o_s + gc[..., 2:3] * o_w
        return o.reshape(B, NSA_Q_CHUNK, NSA_HEADS * HEAD_DIM)

    out = lax.map(chunk, jnp.arange(S // NSA_Q_CHUNK))
    return jnp.moveaxis(out, 0, 1).reshape(B, S, NSA_HEADS * HEAD_DIM)


def _swa_sinks(q, k, v, sinks):
    B, S = q.shape[:2]
    nb = S // SWA_BLOCK
    G = SWA_HEADS // SWA_KV_HEADS
    qb = q.reshape(B, nb, SWA_BLOCK, SWA_KV_HEADS, G, HEAD_DIM)

    def band(t):
        tb = t.reshape(B, nb, SWA_BLOCK, SWA_KV_HEADS, HEAD_DIM)
        prev = jnp.pad(tb, ((0, 0), (1, 0), (0, 0), (0, 0), (0, 0)))[:, :-1]
        return jnp.concatenate([prev, tb], axis=2)

    kb, vb = band(k), band(v)
    s = jnp.einsum('bnqhgd,bnjhd->bnhgqj', qb, kb, preferred_element_type=jnp.float32) * HEAD_DIM ** -0.5
    qi = jnp.arange(SWA_BLOCK)[:, None]
    ji = jnp.arange(2 * SWA_BLOCK)[None, :]
    diff = qi + SWA_BLOCK - ji
    band_ok = (diff >= 0) & (diff < SWA_WINDOW)
    first_ok = ji >= SWA_BLOCK
    blk_first = (jnp.arange(nb) == 0)[:, None, None]
    mask = band_ok[None] & (~blk_first | first_ok[None])
    s = jnp.where(mask[None, :, None, None], s, NEG_INF)
    sink = jnp.broadcast_to(sinks.astype(jnp.float32).reshape(1, 1, SWA_KV_HEADS, G, 1, 1), s.shape[:-1] + (1,))
    p = jax.nn.softmax(jnp.concatenate([s, sink], axis=-1), axis=-1)[..., :-1]
    o = jnp.einsum('bnhgqj,bnjhd->bnqhgd', p.astype(vb.dtype), vb)
    return o.reshape(B, S, SWA_HEADS * HEAD_DIM)


def _mixer(x, cos, sin, w_in, w_out, dw_w, dw_b, cln_g, cln_b, cmp_pos, cmp_w1, cmp_w2, gate_b, sinks):
    B, S, _ = x.shape
    offsets = np.cumsum(IN_SPLITS)[:-1].tolist()
    (a_val, a_gate, q_b, kc, vc, ksl, vsl, kw, vw, g_b, q_c, k_c, v_c) = jnp.split(x @ w_in, offsets, axis=-1)

    def heads(t, n):
        return t.reshape(B, S, n, HEAD_DIM)

    out_a = _conv_module(a_val, a_gate, dw_w, dw_b, cln_g, cln_b)
    qn = _rope(heads(q_b, NSA_HEADS), cos, sin)
    k_cmp = _compress(_rope(heads(kc, NSA_KV_HEADS), cos, sin), cmp_pos[0], cmp_w1[0], cmp_w2[0])
    v_cmp = _compress(heads(vc, NSA_KV_HEADS), cmp_pos[1], cmp_w1[1], cmp_w2[1])
    gates = jax.nn.sigmoid(g_b + gate_b).reshape(B, S, NSA_HEADS, 3)
    out_b = _nsa(qn, k_cmp, v_cmp, _rope(heads(ksl, NSA_KV_HEADS), cos, sin), heads(vsl, NSA_KV_HEADS),
                 _rope(heads(kw, NSA_KV_HEADS), cos, sin), heads(vw, NSA_KV_HEADS), gates)
    out_c = _swa_sinks(_rope(heads(q_c, SWA_HEADS), cos, sin), _rope(heads(k_c, SWA_KV_HEADS), cos, sin),
                       heads(v_c, SWA_KV_HEADS), sinks)
    return jnp.concatenate([out_a, out_b, out_c], axis=-1) @ w_out


def setup_inputs(seed: int = 0) -> dict:
    key = jax.random.key(seed)
    ks = jax.random.split(key, 24)
    L = DEPTH

    def nrm(k, shape, scale):
        return jax.random.normal(k, shape, jnp.float32) * scale

    col_scale = jnp.asarray(np.concatenate([np.full(n, DEEPNORM_BETA if v else 1.0, np.float32) for n, v in zip(IN_SPLITS, IN_VALUE)]))
    return {
        'x': nrm(ks[0], (BATCH, SEQ, D_MODEL), 1.0),
        'positions': jnp.broadcast_to(jnp.arange(SEQ, dtype=jnp.int32), (BATCH, SEQ)),
        'ffn1_w_gu': nrm(ks[1], (L, D_MODEL, 2 * D_FF), D_MODEL ** -0.5),
        'ffn1_w_down': nrm(ks[2], (L, D_FF, D_MODEL), D_FF ** -0.5 * DEEPNORM_BETA),
        'ln1_g': 1.0 + nrm(ks[3], (L, D_MODEL), 0.02),
        'ln1_b': nrm(ks[4], (L, D_MODEL), 0.02),
        'w_in': nrm(ks[5], (L, D_MODEL, D_IN), D_MODEL ** -0.5) * col_scale,
        'conv_dw_w': nrm(ks[6], (L, CONV_WIDTH, CONV_CH), CONV_WIDTH ** -0.5),
        'conv_dw_b': nrm(ks[7], (L, CONV_CH), 0.01),
        'conv_ln_g': 1.0 + nrm(ks[8], (L, CONV_CH), 0.02),
        'conv_ln_b': nrm(ks[9], (L, CONV_CH), 0.02),
        'nsa_cmp_pos': nrm(ks[10], (L, 2, CMP_BLOCK, HEAD_DIM), 0.02),
        'nsa_cmp_w1': nrm(ks[11], (L, 2, CMP_BLOCK * HEAD_DIM, HEAD_DIM), (CMP_BLOCK * HEAD_DIM) ** -0.5),
        'nsa_cmp_w2': nrm(ks[12], (L, 2, HEAD_DIM, HEAD_DIM), HEAD_DIM ** -0.5),
        'nsa_gate_b': nrm(ks[13], (L, NSA_HEADS * 3), 0.1),
        'swa_sinks': nrm(ks[14], (L, SWA_HEADS), 0.5),
        'w_out': nrm(ks[15], (L, D_MIX, D_MODEL), D_MIX ** -0.5 * DEEPNORM_BETA),
        'ln2_g': 1.0 + nrm(ks[16], (L, D_MODEL), 0.02),
        'ln2_b': nrm(ks[17], (L, D_MODEL), 0.02),
        'ffn2_w_gu': nrm(ks[18], (L, D_MODEL, 2 * D_FF), D_MODEL ** -0.5),
        'ffn2_w_down': nrm(ks[19], (L, D_FF, D_MODEL), D_FF ** -0.5 * DEEPNORM_BETA),
        'ln3_g': 1.0 + nrm(ks[20], (L, D_MODEL), 0.02),
        'ln3_b': nrm(ks[21], (L, D_MODEL), 0.02),
    }


def reference(x, positions, ffn1_w_gu, ffn1_w_down, ln1_g, ln1_b, w_in, conv_dw_w, conv_dw_b, conv_ln_g, conv_ln_b,
              nsa_cmp_pos, nsa_cmp_w1, nsa_cmp_w2, nsa_gate_b, swa_sinks, w_out, ln2_g, ln2_b,
              ffn2_w_gu, ffn2_w_down, ln3_g, ln3_b):
    cos, sin = _rope_tables(positions)
    for i in range(DEPTH):
        x = _layernorm(DEEPNORM_ALPHA * x + 0.5 * _swiglu(x, ffn1_w_gu[i], ffn1_w_down[i]), ln1_g[i], ln1_b[i])
        mix = _mixer(x, cos, sin, w_in[i], w_out[i], conv_dw_w[i], conv_dw_b[i], conv_ln_g[i], conv_ln_b[i],
                     nsa_cmp_pos[i], nsa_cmp_w1[i], nsa_cmp_w2[i], nsa_gate_b[i], swa_sinks[i])
        x = _layernorm(DEEPNORM_ALPHA * x + mix, ln2_g[i], ln2_b[i])
        x = _layernorm(DEEPNORM_ALPHA * x + 0.5 * _swiglu(x, ffn2_w_gu[i], ffn2_w_down[i]), ln3_g[i], ln3_b[i])
    return x
```

```python
import functools

import jax
import jax.numpy as jnp
import numpy as np
from jax import lax
from jax.experimental import pallas as pl
from jax.experimental.pallas import tpu as pltpu

F32 = jnp.float32
BF16 = jnp.bfloat16

HEAD_DIM = 128
CONV_CH = 1024
CONV_WIDTH = 31
NSA_HEADS = 12
NSA_KV_HEADS = 4
NSA_GROUP = NSA_HEADS // NSA_KV_HEADS
SWA_HEADS = 12
SWA_KV_HEADS = 2
SWA_GROUP = SWA_HEADS // SWA_KV_HEADS
CMP_BLOCK = 32
CMP_STRIDE = 16
SEL_BLOCK = 64
SEL_SHIFT = 6
SEL_TOPK = 16
NSA_WINDOW = 512
SWA_WINDOW = 128
ROPE_THETA = 10000.0
LN_EPS = 1e-5
DEPTH = 2
DEEPNORM_ALPHA = (2 * DEPTH) ** 0.25
NEG_INF = -1e30
FORCE_SCORE = 1e4
PICKED = -3e38
ATTN_SCALE = HEAD_DIM ** -0.5

V7X_VMEM_BYTES = 64 * 1024 * 1024
LANES = 128

QKV_SWA_Q = 0
QKV_NSA_Q = 12
QKV_KSL = 24
QKV_KWIN = 28
QKV_SWA_K = 32
QKV_ROPE_CHUNKS = 34
QKV_VSL = 34
QKV_VWIN = 38
QKV_SWA_V = 42
QKV_CHUNKS = 44
CONV_GATE_CHUNK = 2 * CONV_CH // LANES


def _cparams(semantics, vmem_mib):
    return pltpu.CompilerParams(dimension_semantics=semantics, vmem_limit_bytes=vmem_mib * 1024 * 1024)


def _tile(n, pref):
    t = min(n, pref)
    while n % t:
        t //= 2
    return t


def _gu_kernel(x_ref, wg_ref, wu_ref, o_ref):
    x = x_ref[...]
    g = jnp.dot(x, wg_ref[...], preferred_element_type=F32)
    u = jnp.dot(x, wu_ref[...], preferred_element_type=F32)
    o_ref[...] = (g * jax.nn.sigmoid(g) * u).astype(o_ref.dtype)


def _ffn_up(xb, wg, wu):
    m, d = xb.shape
    fp = wg.shape[1]
    tm, tn = _tile(m, 1024), _tile(fp, 512)
    return pl.pallas_call(
        _gu_kernel,
        grid=(m // tm, fp // tn),
        in_specs=[pl.BlockSpec((tm, d), lambda i, j: (i, 0)),
                  pl.BlockSpec((d, tn), lambda i, j: (0, j)),
                  pl.BlockSpec((d, tn), lambda i, j: (0, j))],
        out_specs=pl.BlockSpec((tm, tn), lambda i, j: (i, j)),
        out_shape=jax.ShapeDtypeStruct((m, fp), BF16),
        compiler_params=_cparams(("parallel", "arbitrary"), 48),
        name="ffn_up",
    )(xb, wg, wu)


LN_ROWS = 32
MM_COLS = 1024


def _mm_res_ln_kernel(a_ref, w_ref, res_ref, g_ref, b_ref, o_ref, ob_ref, *, scale, nk):
    k = pl.program_id(1)
    a = a_ref[...]
    for c in range(o_ref.shape[1] // MM_COLS):
        cols = slice(c * MM_COLS, (c + 1) * MM_COLS)
        part = jnp.dot(a, w_ref[:, cols], preferred_element_type=F32)

        @pl.when(k == 0)
        def _():
            o_ref[:, cols] = part

        @pl.when(k > 0)
        def _():
            o_ref[:, cols] += part

    @pl.when(k == nk - 1)
    def _():
        gamma = g_ref[...]
        beta = b_ref[...]

        def rows(r, carry):
            sl = pl.ds(pl.multiple_of(r * LN_ROWS, LN_ROWS), LN_ROWS)
            y = DEEPNORM_ALPHA * res_ref[sl, :] + scale * o_ref[sl, :]
            mu = jnp.mean(y, axis=-1, keepdims=True)
            dlt = y - mu
            var = jnp.mean(dlt * dlt, axis=-1, keepdims=True)
            out = dlt * lax.rsqrt(var + LN_EPS) * gamma + beta
            o_ref[sl, :] = out
            ob_ref[sl, :] = out.astype(ob_ref.dtype)
            return carry

        lax.fori_loop(0, o_ref.shape[0] // LN_ROWS, rows, 0)


def _mm_res_ln(a, w, res, gamma, beta, scale):
    m, kdim = a.shape
    d = w.shape[1]
    tm, tk = _tile(m, 512), _tile(kdim, 512)
    nk = kdim // tk
    return pl.pallas_call(
        functools.partial(_mm_res_ln_kernel, scale=scale, nk=nk),
        grid=(m // tm, nk),
        in_specs=[pl.BlockSpec((tm, tk), lambda i, k: (i, k)),
                  pl.BlockSpec((tk, d), lambda i, k: (k, 0)),
                  pl.BlockSpec((tm, d), lambda i, k: (i, 0)),
                  pl.BlockSpec((1, d), lambda i, k: (0, 0)),
                  pl.BlockSpec((1, d), lambda i, k: (0, 0))],
        out_specs=[pl.BlockSpec((tm, d), lambda i, k: (i, 0)),
                   pl.BlockSpec((tm, d), lambda i, k: (i, 0))],
        out_shape=[jax.ShapeDtypeStruct((m, d), F32), jax.ShapeDtypeStruct((m, d), BF16)],
        compiler_params=_cparams(("parallel", "arbitrary"), 56),
        name="mm_res_ln",
    )(a, w, res, gamma.reshape(1, d), beta.reshape(1, d))


def _proj_kernel(x_ref, w_ref, cc_ref, ss_ref, o_ref, *, n_rope_tiles):
    j = pl.program_id(1)
    acc = jnp.dot(x_ref[...], w_ref[...], preferred_element_type=F32)
    tn = acc.shape[1]

    @pl.when(j < n_rope_tiles)
    def _():
        cc = cc_ref[...]
        ss = ss_ref[...]
        for c in range(tn // HEAD_DIM):
            head = acc[:, c * HEAD_DIM:(c + 1) * HEAD_DIM]
            rot = head * cc + pltpu.roll(head, HEAD_DIM // 2, 1) * ss
            o_ref[:, c * HEAD_DIM:(c + 1) * HEAD_DIM] = rot.astype(o_ref.dtype)

    @pl.when(j >= n_rope_tiles)
    def _():
        o_ref[...] = acc.astype(o_ref.dtype)


def _proj(xb, w, cc, ss, n_rope_chunks, out_dtype):
    m, d = xb.shape
    n = w.shape[1]
    tm, tn = _tile(m, 1024), 256
    assert n % tn == 0 and (n_rope_chunks * HEAD_DIM) % tn == 0
    return pl.pallas_call(
        functools.partial(_proj_kernel, n_rope_tiles=n_rope_chunks * HEAD_DIM // tn),
        grid=(m // tm, n // tn),
        in_specs=[pl.BlockSpec((tm, d), lambda i, j: (i, 0)),
                  pl.BlockSpec((d, tn), lambda i, j: (0, j)),
                  pl.BlockSpec((tm, HEAD_DIM), lambda i, j: (i, 0)),
                  pl.BlockSpec((tm, HEAD_DIM), lambda i, j: (i, 0))],
        out_specs=pl.BlockSpec((tm, tn), lambda i, j: (i, j)),
        out_shape=jax.ShapeDtypeStruct((m, n), out_dtype),
        compiler_params=_cparams(("parallel", "arbitrary"), 40),
        name="in_proj",
    )(xb, w, cc, ss)


CONV_HALO = 32
CONV_TS = 128


def _conv_kernel(v_ref, g_ref, vh_ref, gh_ref, w_ref, b_ref, lg_ref, lb_ref, o_ref, ext_ref):
    i = pl.program_id(1)
    ts = v_ref.shape[1]
    halo = vh_ref[0] * jax.nn.sigmoid(gh_ref[0])
    ext_ref[0:CONV_HALO, :] = jnp.where(i > 0, halo, 0.0)
    ext_ref[CONV_HALO:, :] = v_ref[0] * jax.nn.sigmoid(g_ref[0])
    first = CONV_HALO - (CONV_WIDTH - 1)
    cols = []
    for c in range(CONV_CH // LANES):
        lanes = slice(c * LANES, (c + 1) * LANES)
        acc = jnp.zeros((ts, LANES), F32) + b_ref[:, lanes]
        for k in range(CONV_WIDTH):
            acc = acc + w_ref[k:k + 1, lanes] * ext_ref[first + k:first + k + ts, lanes]
        cols.append(acc)
    y = jnp.concatenate(cols, axis=1)
    mu = jnp.mean(y, axis=-1, keepdims=True)
    dlt = y - mu
    var = jnp.mean(dlt * dlt, axis=-1, keepdims=True)
    z = dlt * lax.rsqrt(var + LN_EPS) * lg_ref[...] + lb_ref[...]
    o_ref[0] = (z * jax.nn.sigmoid(z)).astype(o_ref.dtype)


def _conv_module(hconv, dw_w, dw_b, ln_g, ln_b):
    b, s, _ = hconv.shape
    ts = _tile(s, CONV_TS)
    per = ts // CONV_HALO
    w_pad = jnp.concatenate([dw_w, jnp.zeros((CONV_HALO - CONV_WIDTH, CONV_CH), F32)], axis=0)
    vec = lambda a: a.reshape(1, CONV_CH)
    const = lambda shape: pl.BlockSpec(shape, lambda bi, i: (0, 0))
    return pl.pallas_call(
        _conv_kernel,
        grid=(b, s // ts),
        in_specs=[pl.BlockSpec((1, ts, CONV_CH), lambda bi, i: (bi, i, 0)),
                  pl.BlockSpec((1, ts, CONV_CH), lambda bi, i: (bi, i, 1)),
                  pl.BlockSpec((1, CONV_HALO, CONV_CH), lambda bi, i: (bi, jnp.maximum(i * per - 1, 0), 0)),
                  pl.BlockSpec((1, CONV_HALO, CONV_CH), lambda bi, i: (bi, jnp.maximum(i * per - 1, 0), 1)),
                  const((CONV_HALO, CONV_CH)), const((1, CONV_CH)), const((1, CONV_CH)), const((1, CONV_CH))],
        out_specs=pl.BlockSpec((1, ts, CONV_CH), lambda bi, i: (bi, i, 0)),
        out_shape=jax.ShapeDtypeStruct((b, s, CONV_CH), BF16),
        scratch_shapes=[pltpu.VMEM((ts + CONV_HALO, CONV_CH), F32)],
        compiler_params=_cparams(("parallel", "arbitrary"), 32),
        name="conv_module",
    )(hconv, hconv, hconv, hconv, w_pad, vec(dw_b), vec(ln_g), vec(ln_b))


def _compress_kernel(t_ref, pos_ref, w1_ref, w2_ref, o_ref):
    t = t_ref[0, 0, 0]
    n, half = t.shape
    lo = jnp.dot((t + pos_ref[0, 0:1, :]).astype(BF16), w1_ref[0, 0:half, :], preferred_element_type=F32)
    hi = jnp.dot((t + pos_ref[0, 1:2, :]).astype(BF16), w1_ref[0, half:2 * half, :], preferred_element_type=F32)
    h1 = lo + pltpu.roll(hi, n - 1, 0)
    h1 = h1 * jax.nn.sigmoid(h1)
    out = jnp.dot(h1.astype(BF16), w2_ref[0], preferred_element_type=F32)
    row = lax.broadcasted_iota(jnp.int32, out.shape, 0)
    o_ref[0, 0, 0] = jnp.where(row < n - 1, out, 0.0).astype(o_ref.dtype)


def _compress(t4, pos, w1, w2):
    b, _, hkv, n, wide = t4.shape
    return pl.pallas_call(
        _compress_kernel,
        grid=(b, 2, hkv),
        in_specs=[pl.BlockSpec((1, 1, 1, n, wide), lambda bi, kv, h: (bi, kv, h, 0, 0)),
                  pl.BlockSpec((1, 2, wide), lambda bi, kv, h: (kv, 0, 0)),
                  pl.BlockSpec((1, 2 * wide, HEAD_DIM), lambda bi, kv, h: (kv, 0, 0)),
                  pl.BlockSpec((1, HEAD_DIM, HEAD_DIM), lambda bi, kv, h: (kv, 0, 0))],
        out_specs=pl.BlockSpec((1, 1, 1, n, HEAD_DIM), lambda bi, kv, h: (bi, kv, h, 0, 0)),
        out_shape=jax.ShapeDtypeStruct((b, 2, hkv, n, HEAD_DIM), BF16),
        compiler_params=_cparams(("parallel", "parallel", "arbitrary"), 40),
        name="nsa_compress",
    )(t4, pos, w1, w2)


_NT = (((1,), (1,)), ((), ()))


def _stack_heads(q, group):
    return jnp.concatenate([q[:, g * HEAD_DIM:(g + 1) * HEAD_DIM] for g in range(group)], axis=0)


def _online_step(qs, k, v, valid, group, carry):
    m, l, acc = carry
    s = lax.dot_general(qs, k, _NT, preferred_element_type=F32) * ATTN_SCALE
    vg = jnp.concatenate([valid] * group, axis=0)
    sm = jnp.where(vg > 0.0, s, NEG_INF)
    m_new = jnp.maximum(m, jnp.max(sm, axis=-1, keepdims=True))
    alpha = jnp.exp(m - m_new)
    p = jnp.exp(sm - m_new) * vg
    l_new = alpha * l + jnp.sum(p, axis=-1, keepdims=True)
    acc_new = alpha * acc + jnp.dot(p.astype(BF16), v, preferred_element_type=F32)
    return m_new, l_new, acc_new


def _online_init(rows):
    return (jnp.full((rows, 1), NEG_INF, F32), jnp.zeros((rows, 1), F32), jnp.zeros((rows, HEAD_DIM), F32))


NSA_TQ = 256
NSA_TK = 512


def _nsa_kernel(q_ref, ksl_ref, vsl_ref, kw_ref, vw_ref, kc_ref, vc_ref, gl_ref, gb_ref, o_ref, *, tk_sel, tk_win):
    qi = pl.program_id(2)
    tq = q_ref.shape[1]
    seq = ksl_ref.shape[1]
    n_cmp = kc_ref.shape[3]
    n_sel = seq // SEL_BLOCK
    grp = NSA_GROUP
    t0 = qi * tq
    qs = _stack_heads(q_ref[0], grp)
    tpos = t0 + lax.broadcasted_iota(jnp.int32, (tq, 1), 0)

    s_c = lax.dot_general(qs, kc_ref[0, 0, 0], _NT, preferred_element_type=F32) * ATTN_SCALE
    cmp_end = lax.broadcasted_iota(jnp.int32, (1, n_cmp), 1) * CMP_STRIDE + (CMP_BLOCK - 1)
    valid_c = jnp.where(cmp_end <= tpos, 1.0, 0.0)
    vg = jnp.concatenate([valid_c] * grp, axis=0)
    sm = jnp.where(vg > 0.0, s_c, NEG_INF)
    p = jnp.exp(sm - jnp.max(sm, axis=-1, keepdims=True)) * vg
    den = jnp.sum(p, axis=-1, keepdims=True)
    p_c = p / jnp.where(den > 0.0, den, 1.0)
    o_c = jnp.dot(p_c.astype(BF16), vc_ref[0, 0, 0], preferred_element_type=F32)

    p_sum = p_c[0:tq]
    for g in range(1, grp):
        p_sum = p_sum + p_c[g * tq:(g + 1) * tq]
    n_i = lax.broadcasted_iota(jnp.int32, (n_cmp, n_sel), 0)
    j_i = lax.broadcasted_iota(jnp.int32, (n_cmp, n_sel), 1)
    ratio = SEL_BLOCK // CMP_STRIDE
    overlap = jnp.where((n_i >= ratio * j_i - (CMP_BLOCK // CMP_STRIDE - 1)) & (n_i <= ratio * j_i + ratio - 1),
                        1.0, 0.0).astype(BF16)
    p_hi = p_sum.astype(BF16)
    p_lo = (p_sum - p_hi.astype(F32)).astype(BF16)
    imp = (jnp.dot(p_hi, overlap, preferred_element_type=F32) + jnp.dot(p_lo, overlap, preferred_element_type=F32))
    blk = lax.broadcasted_iota(jnp.int32, (1, n_sel), 1)
    t_blk = jnp.right_shift(tpos, SEL_SHIFT)
    forced = (blk == 0) | (blk == t_blk) | (blk == t_blk - 1)
    imp = jnp.where(forced, FORCE_SCORE, imp)
    imp = jnp.where(blk <= t_blk, imp, NEG_INF)

    imp_t = imp.T
    rowf = lax.broadcasted_iota(jnp.int32, (n_sel, tq), 0).astype(F32)
    sel_t = jnp.zeros((n_sel, tq), F32)
    for _ in range(min(SEL_TOPK, n_sel)):
        mx = jnp.max(imp_t, axis=0, keepdims=True)
        idx = jnp.min(jnp.where(imp_t == mx, rowf, float(n_sel)), axis=0, keepdims=True)
        pick = rowf == idx
        sel_t = jnp.where(pick, 1.0, sel_t)
        imp_t = jnp.where(pick, PICKED, imp_t)
    sel = sel_t.T.astype(BF16)

    blocks_per_tile = tk_sel // SEL_BLOCK

    def sel_step(kt, carry):
        k0 = pl.multiple_of(kt * tk_sel, tk_sel)
        jj = lax.broadcasted_iota(jnp.int32, (n_sel, tk_sel), 0)
        cc = lax.broadcasted_iota(jnp.int32, (n_sel, tk_sel), 1)
        expand = jnp.where(jj == kt * blocks_per_tile + jnp.right_shift(cc, SEL_SHIFT), 1.0, 0.0).astype(BF16)
        chosen = jnp.dot(sel, expand, preferred_element_type=F32)
        kpos = k0 + lax.broadcasted_iota(jnp.int32, (1, tk_sel), 1)
        valid = jnp.where((chosen > 0.5) & (kpos <= tpos), 1.0, 0.0)
        return _online_step(qs, ksl_ref[0, pl.ds(k0, tk_sel), :], vsl_ref[0, pl.ds(k0, tk_sel), :], valid, grp, carry)

    n_tiles = (t0 + tq + tk_sel - 1) // tk_sel
    _, l_s, acc_s = lax.fori_loop(0, n_tiles, sel_step, _online_init(grp * tq))
    o_s = acc_s / l_s

    carry = _online_init(grp * tq)
    for j in range((NSA_WINDOW + tq) // tk_win):
        start = t0 - NSA_WINDOW + j * tk_win
        k0 = pl.multiple_of(jnp.maximum(start, 0), tk_win)
        kpos = start + lax.broadcasted_iota(jnp.int32, (1, tk_win), 1)
        valid = jnp.where((kpos >= 0) & (kpos <= tpos) & (kpos > tpos - NSA_WINDOW), 1.0, 0.0)
        carry = _online_step(qs, kw_ref[0, pl.ds(k0, tk_win), :], vw_ref[0, pl.ds(k0, tk_win), :], valid, grp, carry)
    o_w = carry[2] / carry[1]

    gate = jax.nn.sigmoid(gl_ref[0] + gb_ref[...])
    for g in range(grp):
        rows = slice(g * tq, (g + 1) * tq)
        o = (gate[:, 3 * g:3 * g + 1] * o_c[rows] + gate[:, 3 * g + 1:3 * g + 2] * o_s[rows]
             + gate[:, 3 * g + 2:3 * g + 3] * o_w[rows])
        o_ref[0, :, g * HEAD_DIM:(g + 1) * HEAD_DIM] = o.astype(o_ref.dtype)


def _nsa(qkv, kvcmp, hconv, gate_b):
    b, s, _ = qkv.shape
    n = kvcmp.shape[3]
    tq = _tile(s, NSA_TQ)
    tk_sel = _tile(s, NSA_TK)
    tk_win = tq
    assert NSA_WINDOW % tk_win == 0 and tk_sel % SEL_BLOCK == 0
    qw = NSA_GROUP * HEAD_DIM
    full = lambda chunk: pl.BlockSpec((1, s, HEAD_DIM), lambda bi, h, qi: (bi, 0, chunk + h))
    return pl.pallas_call(
        functools.partial(_nsa_kernel, tk_sel=tk_sel, tk_win=tk_win),
        grid=(b, NSA_KV_HEADS, s // tq),
        in_specs=[pl.BlockSpec((1, tq, qw), lambda bi, h, qi: (bi, qi, QKV_NSA_Q // NSA_GROUP + h)),
                  full(QKV_KSL), full(QKV_VSL), full(QKV_KWIN), full(QKV_VWIN),
                  pl.BlockSpec((1, 1, 1, n, HEAD_DIM), lambda bi, h, qi: (bi, 0, h, 0, 0)),
                  pl.BlockSpec((1, 1, 1, n, HEAD_DIM), lambda bi, h, qi: (bi, 1, h, 0, 0)),
                  pl.BlockSpec((1, tq, LANES), lambda bi, h, qi: (bi, qi, CONV_GATE_CHUNK + h)),
                  pl.BlockSpec((1, LANES), lambda bi, h, qi: (0, h))],
        out_specs=pl.BlockSpec((1, tq, qw), lambda bi, h, qi: (bi, qi, h)),
        out_shape=jax.ShapeDtypeStruct((b, s, NSA_HEADS * HEAD_DIM), BF16),
        compiler_params=_cparams(("parallel", "parallel", "arbitrary"), 48),
        name="nsa_attention",
    )(qkv, qkv, qkv, qkv, qkv, kvcmp, kvcmp, hconv, gate_b)


def _swa_kernel(sink_ref, q_ref, kp_ref, kc_ref, vp_ref, vc_ref, o_ref):
    h = pl.program_id(1)
    qi = pl.program_id(2)
    tq = q_ref.shape[1]
    grp = SWA_GROUP
    t0 = qi * tq
    qs = _stack_heads(q_ref[0], grp)
    k = jnp.concatenate([kp_ref[0], kc_ref[0]], axis=0)
    v = jnp.concatenate([vp_ref[0], vc_ref[0]], axis=0)
    tpos = t0 + lax.broadcasted_iota(jnp.int32, (tq, 1), 0)
    kpos = t0 - tq + lax.broadcasted_iota(jnp.int32, (1, 2 * tq), 1)
    valid = jnp.where((kpos >= 0) & (kpos <= tpos) & (kpos > tpos - SWA_WINDOW), 1.0, 0.0)
    s = lax.dot_general(qs, k, _NT, preferred_element_type=F32) * ATTN_SCALE
    vg = jnp.concatenate([valid] * grp, axis=0)
    sink = jnp.concatenate([jnp.full((tq, 1), sink_ref[h * grp + g], F32) for g in range(grp)], axis=0)
    sm = jnp.where(vg > 0.0, s, NEG_INF)
    m = jnp.maximum(jnp.max(sm, axis=-1, keepdims=True), sink)
    p = jnp.exp(sm - m) * vg
    den = jnp.sum(p, axis=-1, keepdims=True) + jnp.exp(sink - m)
    o = jnp.dot(p.astype(BF16), v, preferred_element_type=F32) / den
    for g in range(grp):
        o_ref[0, :, g * HEAD_DIM:(g + 1) * HEAD_DIM] = o[g * tq:(g + 1) * tq].astype(o_ref.dtype)


def _swa(qkv, sinks):
    b, s, _ = qkv.shape
    tq = SWA_WINDOW
    qw = SWA_GROUP * HEAD_DIM
    prev = lambda chunk: pl.BlockSpec((1, tq, HEAD_DIM), lambda bi, h, qi: (bi, jnp.maximum(qi - 1, 0), chunk + h))
    cur = lambda chunk: pl.BlockSpec((1, tq, HEAD_DIM), lambda bi, h, qi: (bi, qi, chunk + h))
    return pl.pallas_call(
        _swa_kernel,
        grid=(b, SWA_KV_HEADS, s // tq),
        in_specs=[pl.BlockSpec(memory_space=pltpu.SMEM),
                  pl.BlockSpec((1, tq, qw), lambda bi, h, qi: (bi, qi, QKV_SWA_Q // SWA_GROUP + h)),
                  prev(QKV_SWA_K), cur(QKV_SWA_K), prev(QKV_SWA_V), cur(QKV_SWA_V)],
        out_specs=pl.BlockSpec((1, tq, qw), lambda bi, h, qi: (bi, qi, h)),
        out_shape=jax.ShapeDtypeStruct((b, s, SWA_HEADS * HEAD_DIM), BF16),
        compiler_params=_cparams(("parallel", "parallel", "arbitrary"), 32),
        name="swa_attention",
    )(sinks, qkv, qkv, qkv, qkv, qkv)


FF_PAD = 1024


def _split_in_proj(w_in, gate_b):
    sizes = (CONV_CH, CONV_CH, NSA_HEADS * HEAD_DIM) + (NSA_KV_HEADS * HEAD_DIM,) * 6 + \
            (NSA_HEADS * 3, SWA_HEADS * HEAD_DIM, SWA_KV_HEADS * HEAD_DIM, SWA_KV_HEADS * HEAD_DIM)
    off = np.concatenate([[0], np.cumsum(sizes)])
    seg = lambda i: w_in[:, off[i]:off[i + 1]]
    a_val, a_gate, q_b, kc, vc, ksl, vsl, kw, vw, g_b, q_c, k_c, v_c = (seg(i) for i in range(13))
    w_qkv = jnp.concatenate([q_c, q_b, ksl, kw, k_c, vsl, vw, v_c], axis=1).astype(BF16)
    w_cmp = jnp.concatenate([kc, vc], axis=1).astype(BF16)
    per_head = NSA_GROUP * 3
    pad_w = jnp.zeros((w_in.shape[0], LANES - per_head), F32)
    pad_b = jnp.zeros((LANES - per_head,), F32)
    gate_w, gate_bias = [], []
    for h in range(NSA_KV_HEADS):
        gate_w += [g_b[:, h * per_head:(h + 1) * per_head], pad_w]
        gate_bias += [gate_b[h * per_head:(h + 1) * per_head], pad_b]
    w_conv = jnp.concatenate([a_val, a_gate] + gate_w, axis=1).astype(BF16)
    return w_qkv, w_cmp, w_conv, jnp.concatenate(gate_bias).reshape(1, NSA_KV_HEADS * LANES)


def _split_ffn(w_gu, w_down):
    ff = w_down.shape[0]
    pad = (-ff) % FF_PAD
    wg = jnp.pad(w_gu[:, :ff], ((0, 0), (0, pad))).astype(BF16)
    wu = jnp.pad(w_gu[:, ff:], ((0, 0), (0, pad))).astype(BF16)
    wd = jnp.pad(w_down, ((0, pad), (0, 0))).astype(BF16)
    return wg, wu, wd


def _ffn_block(x2, xb, w_gu, w_down, ln_g, ln_b):
    wg, wu, wd = _split_ffn(w_gu, w_down)
    hidden = _ffn_up(xb, wg, wu)
    return _mm_res_ln(hidden, wd, x2, ln_g, ln_b, 0.5)


def _mixer_block(x2, xb, cc, ss, batch, w_in, w_out, dw_w, dw_b, cln_g, cln_b, cmp_pos, cmp_w1, cmp_w2,
                 gate_b, sinks, ln_g, ln_b):
    m = x2.shape[0]
    s = m // batch
    w_qkv, w_cmp, w_conv, gate_bias = _split_in_proj(w_in, gate_b)
    qkv = _proj(xb, w_qkv, cc, ss, QKV_ROPE_CHUNKS, BF16).reshape(batch, s, -1)
    hcmp = _proj(xb, w_cmp, cc, ss, NSA_KV_HEADS, F32)
    hconv = _proj(xb, w_conv, cc, ss, 0, F32).reshape(batch, s, -1)
    out_a = _conv_module(hconv, dw_w, dw_b, cln_g, cln_b)
    t4 = hcmp.reshape(batch, s // CMP_STRIDE, CMP_STRIDE, 2, NSA_KV_HEADS, HEAD_DIM)
    t4 = t4.transpose(0, 3, 4, 1, 2, 5).reshape(batch, 2, NSA_KV_HEADS, s // CMP_STRIDE, CMP_STRIDE * HEAD_DIM)
    pos = cmp_pos.reshape(2, 2, CMP_STRIDE * HEAD_DIM)
    kvcmp = _compress(t4, pos, cmp_w1.astype(BF16), cmp_w2.astype(BF16))
    out_b = _nsa(qkv, kvcmp, hconv, gate_bias)
    out_c = _swa(qkv, sinks)
    mixed = jnp.concatenate([out_a, out_b, out_c], axis=-1).reshape(m, -1)
    return _mm_res_ln(mixed, w_out.astype(BF16), x2, ln_g, ln_b, 1.0)


def _rope_tables(positions):
    inv = jnp.power(ROPE_THETA, -jnp.arange(0, HEAD_DIM, 2, dtype=F32) / HEAD_DIM)
    ang = positions.astype(F32).reshape(-1, 1) * inv
    cos, sin = jnp.cos(ang), jnp.sin(ang)
    return jnp.concatenate([cos, cos], axis=-1), jnp.concatenate([-sin, sin], axis=-1)


def kernel(x, positions, ffn1_w_gu, ffn1_w_down, ln1_g, ln1_b, w_in, conv_dw_w, conv_dw_b, conv_ln_g, conv_ln_b,
           nsa_cmp_pos, nsa_cmp_w1, nsa_cmp_w2, nsa_gate_b, swa_sinks, w_out, ln2_g, ln2_b,
           ffn2_w_gu, ffn2_w_down, ln3_g, ln3_b):
    batch, s, d = x.shape
    cc, ss = _rope_tables(positions)
    x2 = x.reshape(batch * s, d)
    xb = x2.astype(BF16)
    for i in range(w_in.shape[0]):
        x2, xb = _ffn_block(x2, xb, ffn1_w_gu[i], ffn1_w_down[i], ln1_g[i], ln1_b[i])
        x2, xb = _mixer_block(x2, xb, cc, ss, batch, w_in[i], w_out[i], conv_dw_w[i], conv_dw_b[i], conv_ln_g[i],
                              conv_ln_b[i], nsa_cmp_pos[i], nsa_cmp_w1[i], nsa_cmp_w2[i], nsa_gate_b[i],
                              swa_sinks[i], ln2_g[i], ln2_b[i])
        x2, xb = _ffn_block(x2, xb, ffn2_w_gu[i], ffn2_w_down[i], ln3_g[i], ln3_b[i])
    return x2.reshape(batch, s, d)
```

```python
import functools

import jax
import jax.numpy as jnp
import numpy as np
from jax import lax
from jax.experimental import pallas as pl
from jax.experimental.pallas import tpu as pltpu

F32 = jnp.float32
BF16 = jnp.bfloat16

HEAD_DIM = 128
CONV_CH = 1024
CONV_WIDTH = 31
NSA_HEADS = 12
NSA_KV_HEADS = 4
NSA_GROUP = NSA_HEADS // NSA_KV_HEADS
SWA_HEADS = 12
SWA_KV_HEADS = 2
SWA_GROUP = SWA_HEADS // SWA_KV_HEADS
CMP_BLOCK = 32
CMP_STRIDE = 16
SEL_BLOCK = 64
SEL_SHIFT = 6
SEL_TOPK = 16
NSA_WINDOW = 512
SWA_WINDOW = 128
ROPE_THETA = 10000.0
LN_EPS = 1e-5
DEPTH = 2
DEEPNORM_ALPHA = (2 * DEPTH) ** 0.25
NEG_INF = -1e30
FORCE_SCORE = 1e4
PICKED = -3e38
ATTN_SCALE = HEAD_DIM ** -0.5
LOG2E = 1.4426950408889634
QUERY_SCALE = ATTN_SCALE * LOG2E

V7X_VMEM_BYTES = 64 * 1024 * 1024
LANES = 128

QKV_SWA_Q = 0
QKV_NSA_Q = 12
QKV_QUERY_CHUNKS = 24
QKV_KSL = 24
QKV_KWIN = 28
QKV_SWA_K = 32
QKV_ROPE_CHUNKS = 34
QKV_VSL = 34
QKV_VWIN = 38
QKV_SWA_V = 42
QKV_CHUNKS = 44
CONV_GATE_CHUNK = 2 * CONV_CH // LANES


def _cparams(semantics, vmem_mib):
    return pltpu.CompilerParams(dimension_semantics=semantics, vmem_limit_bytes=vmem_mib * 1024 * 1024)


def _tile(n, pref):
    t = min(n, pref)
    while n % t:
        t //= 2
    return t


def _gu_kernel(x_ref, wg_ref, wu_ref, o_ref):
    x = x_ref[...]
    g = jnp.dot(x, wg_ref[...], preferred_element_type=F32)
    u = jnp.dot(x, wu_ref[...], preferred_element_type=F32)
    o_ref[...] = (g * jax.nn.sigmoid(g) * u).astype(o_ref.dtype)


def _ffn_up(xb, wg, wu):
    m, d = xb.shape
    fp = wg.shape[1]
    tm, tn = _tile(m, 1024), _tile(fp, 512)
    return pl.pallas_call(
        _gu_kernel,
        grid=(m // tm, fp // tn),
        in_specs=[pl.BlockSpec((tm, d), lambda i, j: (i, 0)),
                  pl.BlockSpec((d, tn), lambda i, j: (0, j)),
                  pl.BlockSpec((d, tn), lambda i, j: (0, j))],
        out_specs=pl.BlockSpec((tm, tn), lambda i, j: (i, j)),
        out_shape=jax.ShapeDtypeStruct((m, fp), BF16),
        compiler_params=_cparams(("parallel", "arbitrary"), 48),
        name="ffn_up",
    )(xb, wg, wu)


LN_ROWS = 32
MM_COLS = 1024


def _mm_res_ln_kernel(a_ref, w_ref, res_ref, g_ref, b_ref, o_ref, ob_ref, *, scale, nk):
    k = pl.program_id(1)

    @pl.when(k == 0)
    def _():
        o_ref[...] = jnp.zeros_like(o_ref)

    a = a_ref[...]
    for c in range(o_ref.shape[1] // MM_COLS):
        cols = slice(c * MM_COLS, (c + 1) * MM_COLS)
        o_ref[:, cols] += jnp.dot(a, w_ref[:, cols], preferred_element_type=F32)

    @pl.when(k == nk - 1)
    def _():
        gamma = g_ref[...]
        beta = b_ref[...]

        def rows(r, carry):
            sl = pl.ds(pl.multiple_of(r * LN_ROWS, LN_ROWS), LN_ROWS)
            y = DEEPNORM_ALPHA * res_ref[sl, :] + scale * o_ref[sl, :]
            mu = jnp.mean(y, axis=-1, keepdims=True)
            dlt = y - mu
            var = jnp.mean(dlt * dlt, axis=-1, keepdims=True)
            out = dlt * lax.rsqrt(var + LN_EPS) * gamma + beta
            o_ref[sl, :] = out
            ob_ref[sl, :] = out.astype(ob_ref.dtype)
            return carry

        lax.fori_loop(0, o_ref.shape[0] // LN_ROWS, rows, 0)


def _mm_res_ln(a, w, res, gamma, beta, scale):
    m, kdim = a.shape
    d = w.shape[1]
    tm, tk = _tile(m, 512), _tile(kdim, 512)
    nk = kdim // tk
    return pl.pallas_call(
        functools.partial(_mm_res_ln_kernel, scale=scale, nk=nk),
        grid=(m // tm, nk),
        in_specs=[pl.BlockSpec((tm, tk), lambda i, k: (i, k)),
                  pl.BlockSpec((tk, d), lambda i, k: (k, 0)),
                  pl.BlockSpec((tm, d), lambda i, k: (i, 0)),
                  pl.BlockSpec((1, d), lambda i, k: (0, 0)),
                  pl.BlockSpec((1, d), lambda i, k: (0, 0))],
        out_specs=[pl.BlockSpec((tm, d), lambda i, k: (i, 0)),
                   pl.BlockSpec((tm, d), lambda i, k: (i, 0))],
        out_shape=[jax.ShapeDtypeStruct((m, d), F32), jax.ShapeDtypeStruct((m, d), BF16)],
        compiler_params=_cparams(("parallel", "arbitrary"), 56),
        name="mm_res_ln",
    )(a, w, res, gamma.reshape(1, d), beta.reshape(1, d))


def _proj_kernel(x_ref, w_ref, rope_ref, o_ref, *, n_query_tiles, n_rope_tiles):
    j = pl.program_id(1)
    acc = jnp.dot(x_ref[...], w_ref[...], preferred_element_type=F32)
    tn = acc.shape[1]

    def rope(table):
        cc = rope_ref[table, :, 0:HEAD_DIM]
        ss = rope_ref[table, :, HEAD_DIM:2 * HEAD_DIM]
        for c in range(tn // HEAD_DIM):
            head = acc[:, c * HEAD_DIM:(c + 1) * HEAD_DIM]
            rot = head * cc + pltpu.roll(head, HEAD_DIM // 2, 1) * ss
            o_ref[:, c * HEAD_DIM:(c + 1) * HEAD_DIM] = rot.astype(o_ref.dtype)

    pl.when(j < n_query_tiles)(lambda: rope(1))
    pl.when((j >= n_query_tiles) & (j < n_rope_tiles))(lambda: rope(0))

    @pl.when(j >= n_rope_tiles)
    def _():
        o_ref[...] = acc.astype(o_ref.dtype)


def _proj(xb, w, rope_tables, n_query_chunks, n_rope_chunks, out_dtype):
    m, d = xb.shape
    n = w.shape[1]
    tm, tn = _tile(m, 1024), 256
    assert n % tn == 0 and (n_rope_chunks * HEAD_DIM) % tn == 0 and (n_query_chunks * HEAD_DIM) % tn == 0
    return pl.pallas_call(
        functools.partial(_proj_kernel, n_query_tiles=n_query_chunks * HEAD_DIM // tn,
                          n_rope_tiles=n_rope_chunks * HEAD_DIM // tn),
        grid=(m // tm, n // tn),
        in_specs=[pl.BlockSpec((tm, d), lambda i, j: (i, 0)),
                  pl.BlockSpec((d, tn), lambda i, j: (0, j)),
                  pl.BlockSpec((2, tm, 2 * HEAD_DIM), lambda i, j: (0, i, 0))],
        out_specs=pl.BlockSpec((tm, tn), lambda i, j: (i, j)),
        out_shape=jax.ShapeDtypeStruct((m, n), out_dtype),
        compiler_params=_cparams(("parallel", "arbitrary"), 40),
        name="in_proj",
    )(xb, w, rope_tables)


CONV_HALO = 32
CONV_TS = 128


def _conv_kernel(v_ref, g_ref, vh_ref, gh_ref, w_ref, b_ref, lg_ref, lb_ref, o_ref, ext_ref):
    i = pl.program_id(1)
    ts = v_ref.shape[1]
    halo = vh_ref[0] * jax.nn.sigmoid(gh_ref[0])
    ext_ref[0:CONV_HALO, :] = jnp.where(i > 0, halo, 0.0)
    ext_ref[CONV_HALO:, :] = v_ref[0] * jax.nn.sigmoid(g_ref[0])
    first = CONV_HALO - (CONV_WIDTH - 1)
    cols = []
    for c in range(CONV_CH // LANES):
        lanes = slice(c * LANES, (c + 1) * LANES)
        acc = jnp.zeros((ts, LANES), F32) + b_ref[:, lanes]
        for k in range(CONV_WIDTH):
            acc = acc + w_ref[k:k + 1, lanes] * ext_ref[first + k:first + k + ts, lanes]
        cols.append(acc)
    y = jnp.concatenate(cols, axis=1)
    mu = jnp.mean(y, axis=-1, keepdims=True)
    dlt = y - mu
    var = jnp.mean(dlt * dlt, axis=-1, keepdims=True)
    z = dlt * lax.rsqrt(var + LN_EPS) * lg_ref[...] + lb_ref[...]
    o_ref[0] = (z * jax.nn.sigmoid(z)).astype(o_ref.dtype)


def _conv_module(hconv, dw_w, dw_b, ln_g, ln_b):
    b, s, _ = hconv.shape
    ts = _tile(s, CONV_TS)
    per = ts // CONV_HALO
    w_pad = jnp.concatenate([dw_w, jnp.zeros((CONV_HALO - CONV_WIDTH, CONV_CH), F32)], axis=0)
    vec = lambda a: a.reshape(1, CONV_CH)
    const = lambda shape: pl.BlockSpec(shape, lambda bi, i: (0, 0))
    return pl.pallas_call(
        _conv_kernel,
        grid=(b, s // ts),
        in_specs=[pl.BlockSpec((1, ts, CONV_CH), lambda bi, i: (bi, i, 0)),
                  pl.BlockSpec((1, ts, CONV_CH), lambda bi, i: (bi, i, 1)),
                  pl.BlockSpec((1, CONV_HALO, CONV_CH), lambda bi, i: (bi, jnp.maximum(i * per - 1, 0), 0)),
                  pl.BlockSpec((1, CONV_HALO, CONV_CH), lambda bi, i: (bi, jnp.maximum(i * per - 1, 0), 1)),
                  const((CONV_HALO, CONV_CH)), const((1, CONV_CH)), const((1, CONV_CH)), const((1, CONV_CH))],
        out_specs=pl.BlockSpec((1, ts, CONV_CH), lambda bi, i: (bi, i, 0)),
        out_shape=jax.ShapeDtypeStruct((b, s, CONV_CH), BF16),
        scratch_shapes=[pltpu.VMEM((ts + CONV_HALO, CONV_CH), F32)],
        compiler_params=_cparams(("parallel", "arbitrary"), 32),
        name="conv_module",
    )(hconv, hconv, hconv, hconv, w_pad, vec(dw_b), vec(ln_g), vec(ln_b))


def _compress_kernel(t_ref, pos_ref, w1_ref, w2_ref, o_ref):
    t = t_ref[0, 0, 0]
    n, half = t.shape
    lo = jnp.dot((t + pos_ref[0, 0:1, :]).astype(BF16), w1_ref[0, 0:half, :], preferred_element_type=F32)
    hi = jnp.dot((t + pos_ref[0, 1:2, :]).astype(BF16), w1_ref[0, half:2 * half, :], preferred_element_type=F32)
    h1 = lo + pltpu.roll(hi, n - 1, 0)
    h1 = h1 * jax.nn.sigmoid(h1)
    out = jnp.dot(h1.astype(BF16), w2_ref[0], preferred_element_type=F32)
    row = lax.broadcasted_iota(jnp.int32, out.shape, 0)
    o_ref[0, 0, 0] = jnp.where(row < n - 1, out, 0.0).astype(o_ref.dtype)


def _compress(t4, pos, w1, w2):
    b, _, hkv, n, wide = t4.shape
    return pl.pallas_call(
        _compress_kernel,
        grid=(b, 2, hkv),
        in_specs=[pl.BlockSpec((1, 1, 1, n, wide), lambda bi, kv, h: (bi, kv, h, 0, 0)),
                  pl.BlockSpec((1, 2, wide), lambda bi, kv, h: (kv, 0, 0)),
                  pl.BlockSpec((1, 2 * wide, HEAD_DIM), lambda bi, kv, h: (kv, 0, 0)),
                  pl.BlockSpec((1, HEAD_DIM, HEAD_DIM), lambda bi, kv, h: (kv, 0, 0))],
        out_specs=pl.BlockSpec((1, 1, 1, n, HEAD_DIM), lambda bi, kv, h: (bi, kv, h, 0, 0)),
        out_shape=jax.ShapeDtypeStruct((b, 2, hkv, n, HEAD_DIM), BF16),
        compiler_params=_cparams(("parallel", "parallel", "arbitrary"), 40),
        name="nsa_compress",
    )(t4, pos, w1, w2)


_NT = (((1,), (1,)), ((), ()))


def _stack_heads(q, group):
    return jnp.concatenate([q[:, g * HEAD_DIM:(g + 1) * HEAD_DIM] for g in range(group)], axis=0)


def _online_step(qs, k, v, bias, carry):
    m, l, acc = carry
    s = lax.dot_general(qs, k, _NT, preferred_element_type=F32)
    if bias is not None:
        tq = bias.shape[0]
        s = jnp.concatenate([s[r:r + tq] + bias for r in range(0, s.shape[0], tq)], axis=0)
    m_new = jnp.maximum(m, jnp.max(s, axis=-1, keepdims=True))
    alpha = jnp.exp2(m - m_new)
    p = jnp.exp2(s - m_new)
    l_new = alpha * l + jnp.sum(p, axis=-1, keepdims=True)
    acc_new = alpha * acc + jnp.dot(p.astype(BF16), v, preferred_element_type=F32)
    return m_new, l_new, acc_new


def _online_init(rows):
    return (jnp.full((rows, 1), NEG_INF, F32), jnp.zeros((rows, 1), F32), jnp.zeros((rows, HEAD_DIM), F32))


NSA_TQ = 256
NSA_TK = 512


def _nsa_kernel(q_ref, ksl_ref, vsl_ref, kw_ref, vw_ref, kc_ref, vc_ref, gl_ref, gb_ref, ex_ref, o_ref, *, tk_sel):
    qi = pl.program_id(2)
    tq = q_ref.shape[1]
    seq = ksl_ref.shape[1]
    n_cmp = kc_ref.shape[3]
    n_sel = seq // SEL_BLOCK
    grp = NSA_GROUP
    t0 = qi * tq
    qs = _stack_heads(q_ref[0], grp)
    tpos = t0 + lax.broadcasted_iota(jnp.int32, (tq, 1), 0)

    cmp_end = lax.broadcasted_iota(jnp.int32, (1, n_cmp), 1) * CMP_STRIDE + (CMP_BLOCK - 1)
    valid_c = cmp_end <= tpos
    s_c = lax.dot_general(qs, kc_ref[0, 0, 0], _NT, preferred_element_type=F32)
    p_parts, p_sum = [], None
    for g in range(grp):
        sm = jnp.where(valid_c, s_c[g * tq:(g + 1) * tq], NEG_INF)
        p = jnp.where(valid_c, jnp.exp2(sm - jnp.max(sm, axis=-1, keepdims=True)), 0.0)
        den = jnp.sum(p, axis=-1, keepdims=True)
        p_c = p / jnp.where(den > 0.0, den, 1.0)
        p_parts.append(p_c.astype(BF16))
        p_sum = p_c if p_sum is None else p_sum + p_c
    o_c = jnp.dot(jnp.concatenate(p_parts, axis=0), vc_ref[0, 0, 0], preferred_element_type=F32)

    n_i = lax.broadcasted_iota(jnp.int32, (n_cmp, n_sel), 0)
    j_i = lax.broadcasted_iota(jnp.int32, (n_cmp, n_sel), 1)
    ratio = SEL_BLOCK // CMP_STRIDE
    overlap = jnp.where((n_i >= ratio * j_i - (CMP_BLOCK // CMP_STRIDE - 1)) & (n_i <= ratio * j_i + ratio - 1),
                        1.0, 0.0).astype(BF16)
    p_hi = p_sum.astype(BF16)
    p_lo = (p_sum - p_hi.astype(F32)).astype(BF16)
    imp = (jnp.dot(p_hi, overlap, preferred_element_type=F32) + jnp.dot(p_lo, overlap, preferred_element_type=F32))
    blk = lax.broadcasted_iota(jnp.int32, (1, n_sel), 1)
    t_blk = jnp.right_shift(tpos, SEL_SHIFT)
    forced = (blk == 0) | (blk == t_blk) | (blk == t_blk - 1)
    imp = jnp.where(forced, FORCE_SCORE, imp)
    imp = jnp.where(blk <= t_blk, imp, NEG_INF)

    imp_t = imp.T
    rowf = lax.broadcasted_iota(jnp.int32, (n_sel, tq), 0).astype(F32)
    sel_t = jnp.zeros((n_sel, tq), F32)
    for _ in range(min(SEL_TOPK, n_sel)):
        mx = jnp.max(imp_t, axis=0, keepdims=True)
        idx = jnp.min(jnp.where(imp_t == mx, rowf, float(n_sel)), axis=0, keepdims=True)
        pick = rowf == idx
        sel_t = jnp.where(pick, 1.0, sel_t)
        imp_t = jnp.where(pick, PICKED, imp_t)
    not_sel = (1.0 - sel_t.T).astype(BF16)

    qa = jnp.concatenate([qs, jnp.concatenate([not_sel] * grp, axis=0)], axis=1)

    def sel_step(kt, carry, causal):
        k0 = pl.multiple_of(kt * tk_sel, tk_sel)
        ka = jnp.concatenate([ksl_ref[0, pl.ds(k0, tk_sel), :], ex_ref[kt]], axis=1)
        bias = None
        if causal:
            kpos = k0 + lax.broadcasted_iota(jnp.int32, (1, tk_sel), 1)
            bias = jnp.where(kpos <= tpos, 0.0, NEG_INF)
        return _online_step(qa, ka, vsl_ref[0, pl.ds(k0, tk_sel), :], bias, carry)

    n_full = t0 // tk_sel
    full_step = functools.partial(sel_step, causal=False)
    n_pairs = n_full // 2
    sel_out = lax.fori_loop(0, n_pairs, lambda i, c: full_step(2 * i + 1, full_step(2 * i, c)),
                            _online_init(grp * tq))
    sel_out = lax.fori_loop(2 * n_pairs, n_full, full_step, sel_out)
    sel_out = sel_step(n_full, sel_out, causal=True)
    o_s = sel_out[2] / sel_out[1]

    span = NSA_WINDOW + tq
    k0 = pl.multiple_of(jnp.maximum(t0 - NSA_WINDOW, 0), tq)
    kpos = k0 + lax.broadcasted_iota(jnp.int32, (1, span), 1)
    bias = jnp.where((kpos <= tpos) & (kpos > tpos - NSA_WINDOW), 0.0, NEG_INF)
    win_out = _online_step(qs, kw_ref[0, pl.ds(k0, span), :], vw_ref[0, pl.ds(k0, span), :], bias,
                           _online_init(grp * tq))

    gate = jax.nn.sigmoid(gl_ref[0] + gb_ref[...])
    o_w = win_out[2] / win_out[1]
    for g in range(grp):
        rows = slice(g * tq, (g + 1) * tq)
        o = (gate[:, 3 * g:3 * g + 1] * o_c[rows] + gate[:, 3 * g + 1:3 * g + 2] * o_s[rows]
             + gate[:, 3 * g + 2:3 * g + 3] * o_w[rows])
        o_ref[0, :, g * HEAD_DIM:(g + 1) * HEAD_DIM] = o.astype(o_ref.dtype)


def _nsa(qkv, kvcmp, hconv, gate_b):
    b, s, _ = qkv.shape
    n = kvcmp.shape[3]
    tq = _tile(s, NSA_TQ)
    tk_sel = _tile(s, NSA_TK)
    assert NSA_WINDOW % tq == 0 and NSA_WINDOW + tq <= s and tk_sel % SEL_BLOCK == 0
    qw = NSA_GROUP * HEAD_DIM
    n_sel = s // SEL_BLOCK
    assert tk_sel % tq == 0
    key_block = np.arange(s).reshape(s // tk_sel, tk_sel, 1) // SEL_BLOCK
    onehot = key_block == np.arange(n_sel).reshape(1, 1, n_sel)
    expand = jnp.asarray(np.where(onehot, NEG_INF, 0.0), BF16)
    full = lambda chunk: pl.BlockSpec((1, s, HEAD_DIM), lambda bi, h, qi: (bi, 0, chunk + h))
    return pl.pallas_call(
        functools.partial(_nsa_kernel, tk_sel=tk_sel),
        grid=(b, NSA_KV_HEADS, s // tq),
        in_specs=[pl.BlockSpec((1, tq, qw), lambda bi, h, qi: (bi, qi, QKV_NSA_Q // NSA_GROUP + h)),
                  full(QKV_KSL), full(QKV_VSL), full(QKV_KWIN), full(QKV_VWIN),
                  pl.BlockSpec((1, 1, 1, n, HEAD_DIM), lambda bi, h, qi: (bi, 0, h, 0, 0)),
                  pl.BlockSpec((1, 1, 1, n, HEAD_DIM), lambda bi, h, qi: (bi, 1, h, 0, 0)),
                  pl.BlockSpec((1, tq, LANES), lambda bi, h, qi: (bi, qi, CONV_GATE_CHUNK + h)),
                  pl.BlockSpec((1, LANES), lambda bi, h, qi: (0, h)),
                  pl.BlockSpec(expand.shape, lambda bi, h, qi: (0, 0, 0))],
        out_specs=pl.BlockSpec((1, tq, qw), lambda bi, h, qi: (bi, qi, h)),
        out_shape=jax.ShapeDtypeStruct((b, s, NSA_HEADS * HEAD_DIM), BF16),
        compiler_params=_cparams(("parallel", "parallel", "arbitrary"), 48),
        name="nsa_attention",
    )(qkv, qkv, qkv, qkv, qkv, kvcmp, kvcmp, hconv, gate_b, expand)


def _swa_kernel(sink_ref, q_ref, kp_ref, kc_ref, vp_ref, vc_ref, o_ref):
    h = pl.program_id(1)
    qi = pl.program_id(2)
    tq = q_ref.shape[1]
    grp = SWA_GROUP
    t0 = qi * tq
    qs = _stack_heads(q_ref[0], grp)
    k = jnp.concatenate([kp_ref[0], kc_ref[0]], axis=0)
    v = jnp.concatenate([vp_ref[0], vc_ref[0]], axis=0)
    tpos = t0 + lax.broadcasted_iota(jnp.int32, (tq, 1), 0)
    kpos = t0 - tq + lax.broadcasted_iota(jnp.int32, (1, 2 * tq), 1)
    valid = jnp.where((kpos >= 0) & (kpos <= tpos) & (kpos > tpos - SWA_WINDOW), 1.0, 0.0)
    s = lax.dot_general(qs, k, _NT, preferred_element_type=F32)
    vg = jnp.concatenate([valid] * grp, axis=0)
    sink = jnp.concatenate([jnp.full((tq, 1), sink_ref[h * grp + g] * LOG2E, F32) for g in range(grp)], axis=0)
    sm = jnp.where(vg > 0.0, s, NEG_INF)
    m = jnp.maximum(jnp.max(sm, axis=-1, keepdims=True), sink)
    p = jnp.exp2(sm - m) * vg
    den = jnp.sum(p, axis=-1, keepdims=True) + jnp.exp2(sink - m)
    o = jnp.dot(p.astype(BF16), v, preferred_element_type=F32) / den
    for g in range(grp):
        o_ref[0, :, g * HEAD_DIM:(g + 1) * HEAD_DIM] = o[g * tq:(g + 1) * tq].astype(o_ref.dtype)


def _swa(qkv, sinks):
    b, s, _ = qkv.shape
    tq = SWA_WINDOW
    qw = SWA_GROUP * HEAD_DIM
    prev = lambda chunk: pl.BlockSpec((1, tq, HEAD_DIM), lambda bi, h, qi: (bi, jnp.maximum(qi - 1, 0), chunk + h))
    cur = lambda chunk: pl.BlockSpec((1, tq, HEAD_DIM), lambda bi, h, qi: (bi, qi, chunk + h))
    return pl.pallas_call(
        _swa_kernel,
        grid=(b, SWA_KV_HEADS, s // tq),
        in_specs=[pl.BlockSpec(memory_space=pltpu.SMEM),
                  pl.BlockSpec((1, tq, qw), lambda bi, h, qi: (bi, qi, QKV_SWA_Q // SWA_GROUP + h)),
                  prev(QKV_SWA_K), cur(QKV_SWA_K), prev(QKV_SWA_V), cur(QKV_SWA_V)],
        out_specs=pl.BlockSpec((1, tq, qw), lambda bi, h, qi: (bi, qi, h)),
        out_shape=jax.ShapeDtypeStruct((b, s, SWA_HEADS * HEAD_DIM), BF16),
        compiler_params=_cparams(("parallel", "parallel", "arbitrary"), 32),
        name="swa_attention",
    )(sinks, qkv, qkv, qkv, qkv, qkv)


FF_PAD = 1024


def _split_in_proj(w_in, gate_b):
    sizes = (CONV_CH, CONV_CH, NSA_HEADS * HEAD_DIM) + (NSA_KV_HEADS * HEAD_DIM,) * 6 + \
            (NSA_HEADS * 3, SWA_HEADS * HEAD_DIM, SWA_KV_HEADS * HEAD_DIM, SWA_KV_HEADS * HEAD_DIM)
    off = np.concatenate([[0], np.cumsum(sizes)])
    seg = lambda i: w_in[:, off[i]:off[i + 1]]
    a_val, a_gate, q_b, kc, vc, ksl, vsl, kw, vw, g_b, q_c, k_c, v_c = (seg(i) for i in range(13))
    w_qkv = jnp.concatenate([q_c, q_b, ksl, kw, k_c, vsl, vw, v_c], axis=1).astype(BF16)
    w_cmp = jnp.concatenate([kc, vc], axis=1).astype(BF16)
    per_head = NSA_GROUP * 3
    pad_w = jnp.zeros((w_in.shape[0], LANES - per_head), F32)
    pad_b = jnp.zeros((LANES - per_head,), F32)
    gate_w, gate_bias = [], []
    for h in range(NSA_KV_HEADS):
        gate_w += [g_b[:, h * per_head:(h + 1) * per_head], pad_w]
        gate_bias += [gate_b[h * per_head:(h + 1) * per_head], pad_b]
    w_conv = jnp.concatenate([a_val, a_gate] + gate_w, axis=1).astype(BF16)
    return w_qkv, w_cmp, w_conv, jnp.concatenate(gate_bias).reshape(1, NSA_KV_HEADS * LANES)


def _split_ffn(w_gu, w_down):
    ff = w_down.shape[0]
    pad = (-ff) % FF_PAD
    wg = jnp.pad(w_gu[:, :ff], ((0, 0), (0, pad))).astype(BF16)
    wu = jnp.pad(w_gu[:, ff:], ((0, 0), (0, pad))).astype(BF16)
    wd = jnp.pad(w_down, ((0, pad), (0, 0))).astype(BF16)
    return wg, wu, wd


def _ffn_block(x2, xb, w_gu, w_down, ln_g, ln_b):
    wg, wu, wd = _split_ffn(w_gu, w_down)
    hidden = _ffn_up(xb, wg, wu)
    return _mm_res_ln(hidden, wd, x2, ln_g, ln_b, 0.5)


def _mixer_block(x2, xb, rope, batch, w_in, w_out, dw_w, dw_b, cln_g, cln_b, cmp_pos, cmp_w1, cmp_w2,
                 gate_b, sinks, ln_g, ln_b):
    m = x2.shape[0]
    s = m // batch
    w_qkv, w_cmp, w_conv, gate_bias = _split_in_proj(w_in, gate_b)
    qkv = _proj(xb, w_qkv, rope, QKV_QUERY_CHUNKS, QKV_ROPE_CHUNKS, BF16).reshape(batch, s, -1)
    hcmp = _proj(xb, w_cmp, rope, 0, NSA_KV_HEADS, F32)
    hconv = _proj(xb, w_conv, rope, 0, 0, F32).reshape(batch, s, -1)
    out_a = _conv_module(hconv, dw_w, dw_b, cln_g, cln_b)
    t4 = hcmp.reshape(batch, s // CMP_STRIDE, CMP_STRIDE, 2, NSA_KV_HEADS, HEAD_DIM)
    t4 = t4.transpose(0, 3, 4, 1, 2, 5).reshape(batch, 2, NSA_KV_HEADS, s // CMP_STRIDE, CMP_STRIDE * HEAD_DIM)
    pos = cmp_pos.reshape(2, 2, CMP_STRIDE * HEAD_DIM)
    kvcmp = _compress(t4, pos, cmp_w1.astype(BF16), cmp_w2.astype(BF16))
    out_b = _nsa(qkv, kvcmp, hconv, gate_bias)
    out_c = _swa(qkv, sinks)
    mixed = jnp.concatenate([out_a, out_b, out_c], axis=-1).reshape(m, -1)
    return _mm_res_ln(mixed, w_out.astype(BF16), x2, ln_g, ln_b, 1.0)


def _rope_tables(positions):
    inv = jnp.power(ROPE_THETA, -jnp.arange(0, HEAD_DIM, 2, dtype=F32) / HEAD_DIM)
    ang = positions.astype(F32).reshape(-1, 1) * inv
    cos, sin = jnp.cos(ang), jnp.sin(ang)
    table = jnp.concatenate([cos, cos, -sin, sin], axis=-1)
    return jnp.stack([table, table * QUERY_SCALE])


def kernel(x, positions, ffn1_w_gu, ffn1_w_down, ln1_g, ln1_b, w_in, conv_dw_w, conv_dw_b, conv_ln_g, conv_ln_b,
           nsa_cmp_pos, nsa_cmp_w1, nsa_cmp_w2, nsa_gate_b, swa_sinks, w_out, ln2_g, ln2_b,
           ffn2_w_gu, ffn2_w_down, ln3_g, ln3_b):
    batch, s, d = x.shape
    rope = _rope_tables(positions)
    x2 = x.reshape(batch * s, d)
    xb = x2.astype(BF16)
    for i in range(w_in.shape[0]):
        x2, xb = _ffn_block(x2, xb, ffn1_w_gu[i], ffn1_w_down[i], ln1_g[i], ln1_b[i])
        x2, xb = _mixer_block(x2, xb, rope, batch, w_in[i], w_out[i], conv_dw_w[i], conv_dw_b[i], conv_ln_g[i],
                              conv_ln_b[i], nsa_cmp_pos[i], nsa_cmp_w1[i], nsa_cmp_w2[i], nsa_gate_b[i],
                              swa_sinks[i], ln2_g[i], ln2_b[i])
        x2, xb = _ffn_block(x2, xb, ffn2_w_gu[i], ffn2_w_down[i], ln3_g[i], ln3_b[i])
    return x2.reshape(batch, s, d)
```

```python
import functools

import jax
import jax.numpy as jnp
import numpy as np
from jax import lax
from jax.experimental import pallas as pl
from jax.experimental.pallas import tpu as pltpu

F32 = jnp.float32
BF16 = jnp.bfloat16

HEAD_DIM = 128
CONV_CH = 1024
CONV_WIDTH = 31
NSA_HEADS = 12
NSA_KV_HEADS = 4
NSA_GROUP = NSA_HEADS // NSA_KV_HEADS
SWA_HEADS = 12
SWA_KV_HEADS = 2
SWA_GROUP = SWA_HEADS // SWA_KV_HEADS
CMP_BLOCK = 32
CMP_STRIDE = 16
SEL_BLOCK = 64
SEL_SHIFT = 6
SEL_TOPK = 16
NSA_WINDOW = 512
SWA_WINDOW = 128
ROPE_THETA = 10000.0
LN_EPS = 1e-5
DEPTH = 2
DEEPNORM_ALPHA = (2 * DEPTH) ** 0.25
NEG_INF = -1e30
FORCE_SCORE = 1e4
PICKED = -3e38
ATTN_SCALE = HEAD_DIM ** -0.5
LOG2E = 1.4426950408889634
QUERY_SCALE = ATTN_SCALE * LOG2E

V7X_VMEM_BYTES = 64 * 1024 * 1024
LANES = 128

QKV_SWA_Q = 0
QKV_NSA_Q = 12
QKV_QUERY_CHUNKS = 24
QKV_KSL = 24
QKV_KWIN = 28
QKV_SWA_K = 32
QKV_ROPE_CHUNKS = 36
QKV_VSL = 36
QKV_VWIN = 40
QKV_SWA_V = 44
QKV_CHUNKS = 48
PROJ_TN = 512
CONV_GATE_CHUNK = 2 * CONV_CH // LANES


def _cparams(semantics, vmem_mib):
    return pltpu.CompilerParams(dimension_semantics=semantics, vmem_limit_bytes=vmem_mib * 1024 * 1024)


def _tile(n, pref):
    t = min(n, pref)
    while n % t:
        t //= 2
    return t


FFN_WBLK = 256
FFN_HIDDEN_ALIGN = 1024


def _gu_kernel(x_ref, wg0_ref, wg1_ref, wu0_ref, wu1_ref, o_ref, *, d_ff):
    j = pl.program_id(1)
    x = x_ref[...]
    wb = wg0_ref.shape[2]
    for half, (wg_ref, wu_ref) in enumerate(((wg0_ref, wu0_ref), (wg1_ref, wu1_ref))):
        g = jnp.dot(x, wg_ref[0], preferred_element_type=F32)
        u = jnp.dot(x, wu_ref[0], preferred_element_type=F32)
        h = g * jax.nn.sigmoid(g) * u
        col = (2 * j + half) * wb + lax.broadcasted_iota(jnp.int32, (1, wb), 1)
        o_ref[:, half * wb:(half + 1) * wb] = jnp.where(col < d_ff, h, 0.0).astype(o_ref.dtype)


def _ffn_up(xb, w_gu, layer, d_ff):
    m, d = xb.shape
    wb = FFN_WBLK
    assert d_ff % wb == 0
    fp = -(-d_ff // FFN_HIDDEN_ALIGN) * FFN_HIDDEN_ALIGN
    up0 = d_ff // wb
    last = 2 * d_ff // wb - 1
    tm = _tile(m, 1024)
    wspec = lambda f: pl.BlockSpec((1, d, wb), lambda i, j: (layer, 0, jnp.minimum(f(j), last)))
    return pl.pallas_call(
        functools.partial(_gu_kernel, d_ff=d_ff),
        grid=(m // tm, fp // (2 * wb)),
        in_specs=[pl.BlockSpec((tm, d), lambda i, j: (i, 0)),
                  wspec(lambda j: 2 * j), wspec(lambda j: 2 * j + 1),
                  wspec(lambda j: up0 + 2 * j), wspec(lambda j: up0 + 2 * j + 1)],
        out_specs=pl.BlockSpec((tm, 2 * wb), lambda i, j: (i, j)),
        out_shape=jax.ShapeDtypeStruct((m, fp), BF16),
        compiler_params=_cparams(("parallel", "arbitrary"), 48),
        name="ffn_up",
    )(xb, w_gu, w_gu, w_gu, w_gu)


LN_ROWS = 32
MM_COLS = 1024


def _mm_res_ln_kernel(a_ref, w_ref, res_ref, g_ref, b_ref, o_ref, ob_ref, *, scale, nk, k_tail):
    k = pl.program_id(1)
    tk = a_ref.shape[1]

    @pl.when(k == 0)
    def _():
        o_ref[...] = jnp.zeros_like(o_ref)

    def accumulate(valid_rows):
        a = a_ref[...]
        for c in range(o_ref.shape[1] // MM_COLS):
            cols = slice(c * MM_COLS, (c + 1) * MM_COLS)
            w = w_ref[0, :, cols]
            if valid_rows < tk:
                w = jnp.where(lax.broadcasted_iota(jnp.int32, w.shape, 0) < valid_rows, w, jnp.zeros_like(w))
            o_ref[:, cols] += jnp.dot(a, w, preferred_element_type=F32)

    if k_tail == tk:
        accumulate(tk)
    else:
        pl.when(k < nk - 1)(lambda: accumulate(tk))
        pl.when(k == nk - 1)(lambda: accumulate(k_tail))

    @pl.when(k == nk - 1)
    def _():
        gamma = g_ref[...]
        beta = b_ref[...]

        def rows(r, carry):
            sl = pl.ds(pl.multiple_of(r * LN_ROWS, LN_ROWS), LN_ROWS)
            y = DEEPNORM_ALPHA * res_ref[sl, :] + scale * o_ref[sl, :]
            mu = jnp.mean(y, axis=-1, keepdims=True)
            dlt = y - mu
            var = jnp.mean(dlt * dlt, axis=-1, keepdims=True)
            out = dlt * lax.rsqrt(var + LN_EPS) * gamma + beta
            o_ref[sl, :] = out
            ob_ref[sl, :] = out.astype(ob_ref.dtype)
            return carry

        lax.fori_loop(0, o_ref.shape[0] // LN_ROWS, rows, 0)


def _mm_res_ln(a, w, layer, res, gamma, beta, scale):
    m, kp = a.shape
    _, k_valid, d = w.shape
    tm, tk = _tile(m, 512), _tile(kp, 1024)
    nk = kp // tk
    assert (nk - 1) * tk < k_valid <= kp
    return pl.pallas_call(
        functools.partial(_mm_res_ln_kernel, scale=scale, nk=nk, k_tail=k_valid - (nk - 1) * tk),
        grid=(m // tm, nk),
        in_specs=[pl.BlockSpec((tm, tk), lambda i, k: (i, k)),
                  pl.BlockSpec((1, tk, d), lambda i, k: (layer, k, 0)),
                  pl.BlockSpec((tm, d), lambda i, k: (i, 0), pipeline_mode=pl.Buffered(1)),
                  pl.BlockSpec((1, d), lambda i, k: (0, 0)),
                  pl.BlockSpec((1, d), lambda i, k: (0, 0))],
        out_specs=[pl.BlockSpec((tm, d), lambda i, k: (i, 0)),
                   pl.BlockSpec((tm, d), lambda i, k: (i, 0))],
        out_shape=[jax.ShapeDtypeStruct((m, d), F32), jax.ShapeDtypeStruct((m, d), BF16)],
        compiler_params=_cparams(("parallel", "arbitrary"), 56),
        name="mm_res_ln",
    )(a, w, res, gamma.reshape(1, d), beta.reshape(1, d))


def _proj_kernel(x_ref, w_ref, rope_ref, o_ref, *, n_query_tiles, n_rope_tiles):
    j = pl.program_id(1)
    acc = jnp.dot(x_ref[...], w_ref[...], preferred_element_type=F32)
    tn = acc.shape[1]

    def rope(table):
        cc = rope_ref[table, :, 0:HEAD_DIM]
        ss = rope_ref[table, :, HEAD_DIM:2 * HEAD_DIM]
        for c in range(tn // HEAD_DIM):
            head = acc[:, c * HEAD_DIM:(c + 1) * HEAD_DIM]
            rot = head * cc + pltpu.roll(head, HEAD_DIM // 2, 1) * ss
            o_ref[:, c * HEAD_DIM:(c + 1) * HEAD_DIM] = rot.astype(o_ref.dtype)

    pl.when(j < n_query_tiles)(lambda: rope(1))
    pl.when((j >= n_query_tiles) & (j < n_rope_tiles))(lambda: rope(0))

    @pl.when(j >= n_rope_tiles)
    def _():
        o_ref[...] = acc.astype(o_ref.dtype)


def _proj(xb, w, rope_tables, n_query_chunks, n_rope_chunks, out_dtype):
    m, d = xb.shape
    n = w.shape[1]
    tm, tn = _tile(m, 1024), PROJ_TN
    assert n % tn == 0 and (n_rope_chunks * HEAD_DIM) % tn == 0 and (n_query_chunks * HEAD_DIM) % tn == 0
    return pl.pallas_call(
        functools.partial(_proj_kernel, n_query_tiles=n_query_chunks * HEAD_DIM // tn,
                          n_rope_tiles=n_rope_chunks * HEAD_DIM // tn),
        grid=(m // tm, n // tn),
        in_specs=[pl.BlockSpec((tm, d), lambda i, j: (i, 0)),
                  pl.BlockSpec((d, tn), lambda i, j: (0, j)),
                  pl.BlockSpec((2, tm, 2 * HEAD_DIM), lambda i, j: (0, i, 0))],
        out_specs=pl.BlockSpec((tm, tn), lambda i, j: (i, j)),
        out_shape=jax.ShapeDtypeStruct((m, n), out_dtype),
        compiler_params=_cparams(("parallel", "arbitrary"), 40),
        name="in_proj",
    )(xb, w, rope_tables)


CONV_HALO = 32
CONV_TS = 128


def _conv_kernel(v_ref, g_ref, vh_ref, gh_ref, w_ref, b_ref, lg_ref, lb_ref, o_ref, ext_ref):
    i = pl.program_id(1)
    ts = v_ref.shape[1]
    halo = vh_ref[0] * jax.nn.sigmoid(gh_ref[0])
    ext_ref[0:CONV_HALO, :] = jnp.where(i > 0, halo, 0.0)
    ext_ref[CONV_HALO:, :] = v_ref[0] * jax.nn.sigmoid(g_ref[0])
    first = CONV_HALO - (CONV_WIDTH - 1)
    cols = []
    for c in range(CONV_CH // LANES):
        lanes = slice(c * LANES, (c + 1) * LANES)
        acc = jnp.zeros((ts, LANES), F32) + b_ref[:, lanes]
        for k in range(CONV_WIDTH):
            acc = acc + w_ref[k:k + 1, lanes] * ext_ref[first + k:first + k + ts, lanes]
        cols.append(acc)
    y = jnp.concatenate(cols, axis=1)
    mu = jnp.mean(y, axis=-1, keepdims=True)
    dlt = y - mu
    var = jnp.mean(dlt * dlt, axis=-1, keepdims=True)
    z = dlt * lax.rsqrt(var + LN_EPS) * lg_ref[...] + lb_ref[...]
    o_ref[0] = (z * jax.nn.sigmoid(z)).astype(o_ref.dtype)


def _conv_module(hconv, dw_w, dw_b, ln_g, ln_b):
    b, s, _ = hconv.shape
    ts = _tile(s, CONV_TS)
    per = ts // CONV_HALO
    w_pad = jnp.concatenate([dw_w, jnp.zeros((CONV_HALO - CONV_WIDTH, CONV_CH), F32)], axis=0)
    vec = lambda a: a.reshape(1, CONV_CH)
    const = lambda shape: pl.BlockSpec(shape, lambda bi, i: (0, 0))
    return pl.pallas_call(
        _conv_kernel,
        grid=(b, s // ts),
        in_specs=[pl.BlockSpec((1, ts, CONV_CH), lambda bi, i: (bi, i, 0)),
                  pl.BlockSpec((1, ts, CONV_CH), lambda bi, i: (bi, i, 1)),
                  pl.BlockSpec((1, CONV_HALO, CONV_CH), lambda bi, i: (bi, jnp.maximum(i * per - 1, 0), 0)),
                  pl.BlockSpec((1, CONV_HALO, CONV_CH), lambda bi, i: (bi, jnp.maximum(i * per - 1, 0), 1)),
                  const((CONV_HALO, CONV_CH)), const((1, CONV_CH)), const((1, CONV_CH)), const((1, CONV_CH))],
        out_specs=pl.BlockSpec((1, ts, CONV_CH), lambda bi, i: (bi, i, 0)),
        out_shape=jax.ShapeDtypeStruct((b, s, CONV_CH), BF16),
        scratch_shapes=[pltpu.VMEM((ts + CONV_HALO, CONV_CH), F32)],
        compiler_params=_cparams(("parallel", "arbitrary"), 32),
        name="conv_module",
    )(hconv, hconv, hconv, hconv, w_pad, vec(dw_b), vec(ln_g), vec(ln_b))


def _compress_kernel(t_ref, pos_ref, w1_ref, w2_ref, o_ref):
    t = t_ref[0, 0, 0]
    n, half = t.shape
    lo = jnp.dot((t + pos_ref[0, 0:1, :]).astype(BF16), w1_ref[0, 0:half, :], preferred_element_type=F32)
    hi = jnp.dot((t + pos_ref[0, 1:2, :]).astype(BF16), w1_ref[0, half:2 * half, :], preferred_element_type=F32)
    h1 = lo + pltpu.roll(hi, n - 1, 0)
    h1 = h1 * jax.nn.sigmoid(h1)
    out = jnp.dot(h1.astype(BF16), w2_ref[0], preferred_element_type=F32)
    row = lax.broadcasted_iota(jnp.int32, out.shape, 0)
    o_ref[0, 0, 0] = jnp.where(row < n - 1, out, 0.0).astype(o_ref.dtype)


def _compress(t4, pos, w1, w2):
    b, _, hkv, n, wide = t4.shape
    return pl.pallas_call(
        _compress_kernel,
        grid=(b, 2, hkv),
        in_specs=[pl.BlockSpec((1, 1, 1, n, wide), lambda bi, kv, h: (bi, kv, h, 0, 0)),
                  pl.BlockSpec((1, 2, wide), lambda bi, kv, h: (kv, 0, 0)),
                  pl.BlockSpec((1, 2 * wide, HEAD_DIM), lambda bi, kv, h: (kv, 0, 0)),
                  pl.BlockSpec((1, HEAD_DIM, HEAD_DIM), lambda bi, kv, h: (kv, 0, 0))],
        out_specs=pl.BlockSpec((1, 1, 1, n, HEAD_DIM), lambda bi, kv, h: (bi, kv, h, 0, 0)),
        out_shape=jax.ShapeDtypeStruct((b, 2, hkv, n, HEAD_DIM), BF16),
        compiler_params=_cparams(("parallel", "parallel", "arbitrary"), 40),
        name="nsa_compress",
    )(t4, pos, w1, w2)


_NT = (((1,), (1,)), ((), ()))


def _stack_heads(q, group):
    return jnp.concatenate([q[:, g * HEAD_DIM:(g + 1) * HEAD_DIM] for g in range(group)], axis=0)


def _online_step(qs, k, v, bias, carry):
    m, l, acc = carry
    s = lax.dot_general(qs, k, _NT, preferred_element_type=F32)
    if bias is not None:
        tq = bias.shape[0]
        s = jnp.concatenate([s[r:r + tq] + bias for r in range(0, s.shape[0], tq)], axis=0)
    m_new = jnp.maximum(m, jnp.max(s, axis=-1, keepdims=True))
    alpha = jnp.exp2(m - m_new)
    p = jnp.exp2(s - m_new)
    l_new = alpha * l + jnp.sum(p, axis=-1, keepdims=True)
    acc_new = alpha * acc + jnp.dot(p.astype(BF16), v, preferred_element_type=F32)
    return m_new, l_new, acc_new


def _online_init(rows):
    return (jnp.full((rows, 1), NEG_INF, F32), jnp.zeros((rows, 1), F32), jnp.zeros((rows, HEAD_DIM), F32))


NSA_TQ = 256
NSA_TK = 512


def _nsa_kernel(q_ref, ksl_ref, vsl_ref, kw_ref, vw_ref, kc_ref, vc_ref, gl_ref, gb_ref, ex_ref, o_ref, *, tk_sel):
    qi = pl.program_id(2)
    tq = q_ref.shape[1]
    seq = ksl_ref.shape[1]
    n_cmp = kc_ref.shape[3]
    n_sel = seq // SEL_BLOCK
    grp = NSA_GROUP
    t0 = qi * tq
    qs = _stack_heads(q_ref[0], grp)
    tpos = t0 + lax.broadcasted_iota(jnp.int32, (tq, 1), 0)

    cmp_end = lax.broadcasted_iota(jnp.int32, (1, n_cmp), 1) * CMP_STRIDE + (CMP_BLOCK - 1)
    valid_c = cmp_end <= tpos
    s_c = lax.dot_general(qs, kc_ref[0, 0, 0], _NT, preferred_element_type=F32)
    p_parts, p_sum = [], None
    for g in range(grp):
        sm = jnp.where(valid_c, s_c[g * tq:(g + 1) * tq], NEG_INF)
        p = jnp.where(valid_c, jnp.exp2(sm - jnp.max(sm, axis=-1, keepdims=True)), 0.0)
        den = jnp.sum(p, axis=-1, keepdims=True)
        p_c = p / jnp.where(den > 0.0, den, 1.0)
        p_parts.append(p_c.astype(BF16))
        p_sum = p_c if p_sum is None else p_sum + p_c
    o_c = jnp.dot(jnp.concatenate(p_parts, axis=0), vc_ref[0, 0, 0], preferred_element_type=F32)

    n_i = lax.broadcasted_iota(jnp.int32, (n_cmp, n_sel), 0)
    j_i = lax.broadcasted_iota(jnp.int32, (n_cmp, n_sel), 1)
    ratio = SEL_BLOCK // CMP_STRIDE
    overlap = jnp.where((n_i >= ratio * j_i - (CMP_BLOCK // CMP_STRIDE - 1)) & (n_i <= ratio * j_i + ratio - 1),
                        1.0, 0.0).astype(BF16)
    p_hi = p_sum.astype(BF16)
    p_lo = (p_sum - p_hi.astype(F32)).astype(BF16)
    imp = (jnp.dot(p_hi, overlap, preferred_element_type=F32) + jnp.dot(p_lo, overlap, preferred_element_type=F32))
    blk = lax.broadcasted_iota(jnp.int32, (1, n_sel), 1)
    t_blk = jnp.right_shift(tpos, SEL_SHIFT)
    forced = (blk == 0) | (blk == t_blk) | (blk == t_blk - 1)
    imp = jnp.where(forced, FORCE_SCORE, imp)
    imp = jnp.where(blk <= t_blk, imp, NEG_INF)

    imp_t = imp.T
    rowf = lax.broadcasted_iota(jnp.int32, (n_sel, tq), 0).astype(F32)
    sel_t = jnp.zeros((n_sel, tq), F32)
    for _ in range(min(SEL_TOPK, n_sel)):
        mx = jnp.max(imp_t, axis=0, keepdims=True)
        idx = jnp.min(jnp.where(imp_t == mx, rowf, float(n_sel)), axis=0, keepdims=True)
        pick = rowf == idx
        sel_t = jnp.where(pick, 1.0, sel_t)
        imp_t = jnp.where(pick, PICKED, imp_t)
    not_sel = (1.0 - sel_t.T).astype(BF16)

    qa = jnp.concatenate([qs, jnp.concatenate([not_sel] * grp, axis=0)], axis=1)

    def sel_step(kt, carry, causal):
        k0 = pl.multiple_of(kt * tk_sel, tk_sel)
        ka = jnp.concatenate([ksl_ref[0, pl.ds(k0, tk_sel), :], ex_ref[kt]], axis=1)
        bias = None
        if causal:
            kpos = k0 + lax.broadcasted_iota(jnp.int32, (1, tk_sel), 1)
            bias = jnp.where(kpos <= tpos, 0.0, NEG_INF)
        return _online_step(qa, ka, vsl_ref[0, pl.ds(k0, tk_sel), :], bias, carry)

    n_full = t0 // tk_sel
    full_step = functools.partial(sel_step, causal=False)
    n_pairs = n_full // 2
    sel_out = lax.fori_loop(0, n_pairs, lambda i, c: full_step(2 * i + 1, full_step(2 * i, c)),
                            _online_init(grp * tq))
    sel_out = lax.fori_loop(2 * n_pairs, n_full, full_step, sel_out)
    sel_out = sel_step(n_full, sel_out, causal=True)
    o_s = sel_out[2] / sel_out[1]

    span = NSA_WINDOW + tq
    k0 = pl.multiple_of(jnp.maximum(t0 - NSA_WINDOW, 0), tq)
    kpos = k0 + lax.broadcasted_iota(jnp.int32, (1, span), 1)
    bias = jnp.where((kpos <= tpos) & (kpos > tpos - NSA_WINDOW), 0.0, NEG_INF)
    win_out = _online_step(qs, kw_ref[0, pl.ds(k0, span), :], vw_ref[0, pl.ds(k0, span), :], bias,
                           _online_init(grp * tq))

    gate = jax.nn.sigmoid(gl_ref[0] + gb_ref[...])
    o_w = win_out[2] / win_out[1]
    for g in range(grp):
        rows = slice(g * tq, (g + 1) * tq)
        o = (gate[:, 3 * g:3 * g + 1] * o_c[rows] + gate[:, 3 * g + 1:3 * g + 2] * o_s[rows]
             + gate[:, 3 * g + 2:3 * g + 3] * o_w[rows])
        o_ref[0, :, g * HEAD_DIM:(g + 1) * HEAD_DIM] = o.astype(o_ref.dtype)


def _nsa(qkv, kvcmp, hconv, gate_b):
    b, s, _ = qkv.shape
    n = kvcmp.shape[3]
    tq = _tile(s, NSA_TQ)
    tk_sel = _tile(s, NSA_TK)
    assert NSA_WINDOW % tq == 0 and NSA_WINDOW + tq <= s and tk_sel % SEL_BLOCK == 0
    qw = NSA_GROUP * HEAD_DIM
    n_sel = s // SEL_BLOCK
    assert tk_sel % tq == 0
    key_block = np.arange(s).reshape(s // tk_sel, tk_sel, 1) // SEL_BLOCK
    onehot = key_block == np.arange(n_sel).reshape(1, 1, n_sel)
    expand = jnp.asarray(np.where(onehot, NEG_INF, 0.0), BF16)
    full = lambda chunk: pl.BlockSpec((1, s, HEAD_DIM), lambda bi, h, qi: (bi, 0, chunk + h))
    return pl.pallas_call(
        functools.partial(_nsa_kernel, tk_sel=tk_sel),
        grid=(b, NSA_KV_HEADS, s // tq),
        in_specs=[pl.BlockSpec((1, tq, qw), lambda bi, h, qi: (bi, qi, QKV_NSA_Q // NSA_GROUP + h)),
                  full(QKV_KSL), full(QKV_VSL), full(QKV_KWIN), full(QKV_VWIN),
                  pl.BlockSpec((1, 1, 1, n, HEAD_DIM), lambda bi, h, qi: (bi, 0, h, 0, 0)),
                  pl.BlockSpec((1, 1, 1, n, HEAD_DIM), lambda bi, h, qi: (bi, 1, h, 0, 0)),
                  pl.BlockSpec((1, tq, LANES), lambda bi, h, qi: (bi, qi, CONV_GATE_CHUNK + h)),
                  pl.BlockSpec((1, LANES), lambda bi, h, qi: (0, h)),
                  pl.BlockSpec(expand.shape, lambda bi, h, qi: (0, 0, 0))],
        out_specs=pl.BlockSpec((1, tq, qw), lambda bi, h, qi: (bi, qi, h)),
        out_shape=jax.ShapeDtypeStruct((b, s, NSA_HEADS * HEAD_DIM), BF16),
        compiler_params=_cparams(("parallel", "parallel", "arbitrary"), 48),
        name="nsa_attention",
    )(qkv, qkv, qkv, qkv, qkv, kvcmp, kvcmp, hconv, gate_b, expand)


def _swa_kernel(sink_ref, q_ref, kp_ref, kc_ref, vp_ref, vc_ref, o_ref):
    h = pl.program_id(1)
    qi = pl.program_id(2)
    tq = q_ref.shape[1]
    grp = SWA_GROUP
    t0 = qi * tq
    qs = _stack_heads(q_ref[0], grp)
    k = jnp.concatenate([kp_ref[0], kc_ref[0]], axis=0)
    v = jnp.concatenate([vp_ref[0], vc_ref[0]], axis=0)
    tpos = t0 + lax.broadcasted_iota(jnp.int32, (tq, 1), 0)
    kpos = t0 - tq + lax.broadcasted_iota(jnp.int32, (1, 2 * tq), 1)
    valid = jnp.where((kpos >= 0) & (kpos <= tpos) & (kpos > tpos - SWA_WINDOW), 1.0, 0.0)
    s = lax.dot_general(qs, k, _NT, preferred_element_type=F32)
    vg = jnp.concatenate([valid] * grp, axis=0)
    sink = jnp.concatenate([jnp.full((tq, 1), sink_ref[h * grp + g] * LOG2E, F32) for g in range(grp)], axis=0)
    sm = jnp.where(vg > 0.0, s, NEG_INF)
    m = jnp.maximum(jnp.max(sm, axis=-1, keepdims=True), sink)
    p = jnp.exp2(sm - m) * vg
    den = jnp.sum(p, axis=-1, keepdims=True) + jnp.exp2(sink - m)
    o = jnp.dot(p.astype(BF16), v, preferred_element_type=F32) / den
    for g in range(grp):
        o_ref[0, :, g * HEAD_DIM:(g + 1) * HEAD_DIM] = o[g * tq:(g + 1) * tq].astype(o_ref.dtype)


def _swa(qkv, sinks):
    b, s, _ = qkv.shape
    tq = SWA_WINDOW
    qw = SWA_GROUP * HEAD_DIM
    prev = lambda chunk: pl.BlockSpec((1, tq, HEAD_DIM), lambda bi, h, qi: (bi, jnp.maximum(qi - 1, 0), chunk + h))
    cur = lambda chunk: pl.BlockSpec((1, tq, HEAD_DIM), lambda bi, h, qi: (bi, qi, chunk + h))
    return pl.pallas_call(
        _swa_kernel,
        grid=(b, SWA_KV_HEADS, s // tq),
        in_specs=[pl.BlockSpec(memory_space=pltpu.SMEM),
                  pl.BlockSpec((1, tq, qw), lambda bi, h, qi: (bi, qi, QKV_SWA_Q // SWA_GROUP + h)),
                  prev(QKV_SWA_K), cur(QKV_SWA_K), prev(QKV_SWA_V), cur(QKV_SWA_V)],
        out_specs=pl.BlockSpec((1, tq, qw), lambda bi, h, qi: (bi, qi, h)),
        out_shape=jax.ShapeDtypeStruct((b, s, SWA_HEADS * HEAD_DIM), BF16),
        compiler_params=_cparams(("parallel", "parallel", "arbitrary"), 32),
        name="swa_attention",
    )(sinks, qkv, qkv, qkv, qkv, qkv)


def _split_in_proj(w_in, gate_b):
    sizes = (CONV_CH, CONV_CH, NSA_HEADS * HEAD_DIM) + (NSA_KV_HEADS * HEAD_DIM,) * 6 + \
            (NSA_HEADS * 3, SWA_HEADS * HEAD_DIM, SWA_KV_HEADS * HEAD_DIM, SWA_KV_HEADS * HEAD_DIM)
    off = np.concatenate([[0], np.cumsum(sizes)])
    seg = lambda i: w_in[:, off[i]:off[i + 1]]
    a_val, a_gate, q_b, kc, vc, ksl, vsl, kw, vw, g_b, q_c, k_c, v_c = (seg(i) for i in range(13))
    gap = jnp.zeros((w_in.shape[0], 2 * HEAD_DIM), F32)
    w_qkv = jnp.concatenate([q_c, q_b, ksl, kw, k_c, gap, vsl, vw, v_c, gap], axis=1).astype(BF16)
    assert w_qkv.shape[1] == QKV_CHUNKS * HEAD_DIM
    w_cmp = jnp.concatenate([kc, vc], axis=1).astype(BF16)
    per_head = NSA_GROUP * 3
    pad_w = jnp.zeros((w_in.shape[0], LANES - per_head), F32)
    pad_b = jnp.zeros((LANES - per_head,), F32)
    gate_w, gate_bias = [], []
    for h in range(NSA_KV_HEADS):
        gate_w += [g_b[:, h * per_head:(h + 1) * per_head], pad_w]
        gate_bias += [gate_b[h * per_head:(h + 1) * per_head], pad_b]
    w_conv = jnp.concatenate([a_val, a_gate] + gate_w, axis=1).astype(BF16)
    return w_qkv, w_cmp, w_conv, jnp.concatenate(gate_bias).reshape(1, NSA_KV_HEADS * LANES)


def _ffn_block(x2, xb, w_gu, w_down, layer, ln_g, ln_b):
    hidden = _ffn_up(xb, w_gu, layer, w_down.shape[1])
    return _mm_res_ln(hidden, w_down, layer, x2, ln_g, ln_b, 0.5)


def _mixer_block(x2, xb, rope, batch, w_in, w_out, layer, dw_w, dw_b, cln_g, cln_b, cmp_pos, cmp_w1, cmp_w2,
                 gate_b, sinks, ln_g, ln_b):
    m = x2.shape[0]
    s = m // batch
    w_qkv, w_cmp, w_conv, gate_bias = _split_in_proj(w_in, gate_b)
    qkv = _proj(xb, w_qkv, rope, QKV_QUERY_CHUNKS, QKV_ROPE_CHUNKS, BF16).reshape(batch, s, -1)
    hcmp = _proj(xb, w_cmp, rope, 0, NSA_KV_HEADS, F32)
    hconv = _proj(xb, w_conv, rope, 0, 0, F32).reshape(batch, s, -1)
    out_a = _conv_module(hconv, dw_w, dw_b, cln_g, cln_b)
    t4 = hcmp.reshape(batch, s // CMP_STRIDE, CMP_STRIDE, 2, NSA_KV_HEADS, HEAD_DIM)
    t4 = t4.transpose(0, 3, 4, 1, 2, 5).reshape(batch, 2, NSA_KV_HEADS, s // CMP_STRIDE, CMP_STRIDE * HEAD_DIM)
    pos = cmp_pos.reshape(2, 2, CMP_STRIDE * HEAD_DIM)
    kvcmp = _compress(t4, pos, cmp_w1.astype(BF16), cmp_w2.astype(BF16))
    out_b = _nsa(qkv, kvcmp, hconv, gate_bias)
    out_c = _swa(qkv, sinks)
    mixed = jnp.concatenate([out_a, out_b, out_c], axis=-1).reshape(m, -1)
    return _mm_res_ln(mixed, w_out, layer, x2, ln_g, ln_b, 1.0)


def _rope_tables(positions):
    inv = jnp.power(ROPE_THETA, -jnp.arange(0, HEAD_DIM, 2, dtype=F32) / HEAD_DIM)
    ang = positions.astype(F32).reshape(-1, 1) * inv
    cos, sin = jnp.cos(ang), jnp.sin(ang)
    table = jnp.concatenate([cos, cos, -sin, sin], axis=-1)
    return jnp.stack([table, table * QUERY_SCALE])


def kernel(x, positions, ffn1_w_gu, ffn1_w_down, ln1_g, ln1_b, w_in, conv_dw_w, conv_dw_b, conv_ln_g, conv_ln_b,
           nsa_cmp_pos, nsa_cmp_w1, nsa_cmp_w2, nsa_gate_b, swa_sinks, w_out, ln2_g, ln2_b,
           ffn2_w_gu, ffn2_w_down, ln3_g, ln3_b):
    batch, s, d = x.shape
    rope = _rope_tables(positions)
    x2 = x.reshape(batch * s, d)
    xb = x2.astype(BF16)
    gu1, down1, gu2, down2, out_w = (w.astype(BF16) for w in (ffn1_w_gu, ffn1_w_down, ffn2_w_gu, ffn2_w_down, w_out))
    for i in range(w_in.shape[0]):
        x2, xb = _ffn_block(x2, xb, gu1, down1, i, ln1_g[i], ln1_b[i])
        x2, xb = _mixer_block(x2, xb, rope, batch, w_in[i], out_w, i, conv_dw_w[i], conv_dw_b[i], conv_ln_g[i],
                              conv_ln_b[i], nsa_cmp_pos[i], nsa_cmp_w1[i], nsa_cmp_w2[i], nsa_gate_b[i],
                              swa_sinks[i], ln2_g[i], ln2_b[i])
        x2, xb = _ffn_block(x2, xb, gu2, down2, i, ln3_g[i], ln3_b[i])
    return x2.reshape(batch, s, d)
```

```python
import functools

import jax
import jax.numpy as jnp
import numpy as np
from jax import lax
from jax.experimental import pallas as pl
from jax.experimental.pallas import tpu as pltpu

F32 = jnp.float32
BF16 = jnp.bfloat16

HEAD_DIM = 128
CONV_CH = 1024
CONV_WIDTH = 31
NSA_HEADS = 12
NSA_KV_HEADS = 4
NSA_GROUP = NSA_HEADS // NSA_KV_HEADS
SWA_HEADS = 12
SWA_KV_HEADS = 2
SWA_GROUP = SWA_HEADS // SWA_KV_HEADS
CMP_BLOCK = 32
CMP_STRIDE = 16
SEL_BLOCK = 64
SEL_SHIFT = 6
SEL_TOPK = 16
NSA_WINDOW = 512
SWA_WINDOW = 128
ROPE_THETA = 10000.0
LN_EPS = 1e-5
DEPTH = 2
DEEPNORM_ALPHA = (2 * DEPTH) ** 0.25
NEG_INF = -1e30
FORCE_SCORE = 1e4
PICKED = -3e38
ATTN_SCALE = HEAD_DIM ** -0.5
LOG2E = 1.4426950408889634
QUERY_SCALE = ATTN_SCALE * LOG2E

V7X_VMEM_BYTES = 64 * 1024 * 1024
LANES = 128
SUBLANES = 8

QKV_SWA_Q = 0
QKV_NSA_Q = 12
QKV_QUERY_CHUNKS = 24
QKV_KSL = 24
QKV_KWIN = 28
QKV_SWA_K = 32
QKV_ROPE_CHUNKS = 36
QKV_VSL = 36
QKV_VWIN = 40
QKV_SWA_V = 44
QKV_CHUNKS = 48
PROJ_TN = 512
CONV_GATE_CHUNK = 2 * CONV_CH // LANES


def _cparams(semantics, vmem_mib):
    return pltpu.CompilerParams(dimension_semantics=semantics, vmem_limit_bytes=vmem_mib * 1024 * 1024)


def _tile(n, pref):
    t = min(n, pref)
    while n % t:
        t //= 2
    return t


FFN_WBLK = 256
FFN_HIDDEN_ALIGN = 1024


def _gu_kernel(x_ref, wg0_ref, wg1_ref, wu0_ref, wu1_ref, o_ref, *, d_ff):
    j = pl.program_id(1)
    x = x_ref[...]
    wb = wg0_ref.shape[2]
    for half, (wg_ref, wu_ref) in enumerate(((wg0_ref, wu0_ref), (wg1_ref, wu1_ref))):
        g = jnp.dot(x, wg_ref[0], preferred_element_type=F32)
        u = jnp.dot(x, wu_ref[0], preferred_element_type=F32)
        h = g * jax.nn.sigmoid(g) * u
        col = (2 * j + half) * wb + lax.broadcasted_iota(jnp.int32, (1, wb), 1)
        o_ref[:, half * wb:(half + 1) * wb] = jnp.where(col < d_ff, h, 0.0).astype(o_ref.dtype)


def _ffn_up(xb, w_gu, layer, d_ff):
    m, d = xb.shape
    wb = FFN_WBLK
    assert d_ff % wb == 0
    fp = -(-d_ff // FFN_HIDDEN_ALIGN) * FFN_HIDDEN_ALIGN
    up0 = d_ff // wb
    last = 2 * d_ff // wb - 1
    tm = _tile(m, 1024)
    wspec = lambda f: pl.BlockSpec((1, d, wb), lambda i, j: (layer, 0, jnp.minimum(f(j), last)))
    return pl.pallas_call(
        functools.partial(_gu_kernel, d_ff=d_ff),
        grid=(m // tm, fp // (2 * wb)),
        in_specs=[pl.BlockSpec((tm, d), lambda i, j: (i, 0)),
                  wspec(lambda j: 2 * j), wspec(lambda j: 2 * j + 1),
                  wspec(lambda j: up0 + 2 * j), wspec(lambda j: up0 + 2 * j + 1)],
        out_specs=pl.BlockSpec((tm, 2 * wb), lambda i, j: (i, j)),
        out_shape=jax.ShapeDtypeStruct((m, fp), BF16),
        compiler_params=_cparams(("parallel", "arbitrary"), 48),
        name="ffn_up",
    )(xb, w_gu, w_gu, w_gu, w_gu)


LN_ROWS = 32
MM_COLS = 1024


def _mm_res_ln_kernel(a_ref, w_ref, res_ref, g_ref, b_ref, o_ref, ob_ref, *, scale, nk, k_tail):
    k = pl.program_id(1)
    tk = a_ref.shape[1]

    @pl.when(k == 0)
    def _():
        o_ref[...] = jnp.zeros_like(o_ref)

    def accumulate(valid_rows):
        a = a_ref[...]
        for c in range(o_ref.shape[1] // MM_COLS):
            cols = slice(c * MM_COLS, (c + 1) * MM_COLS)
            w = w_ref[0, :, cols]
            if valid_rows < tk:
                w = jnp.where(lax.broadcasted_iota(jnp.int32, w.shape, 0) < valid_rows, w, jnp.zeros_like(w))
            o_ref[:, cols] += jnp.dot(a, w, preferred_element_type=F32)

    if k_tail == tk:
        accumulate(tk)
    else:
        pl.when(k < nk - 1)(lambda: accumulate(tk))
        pl.when(k == nk - 1)(lambda: accumulate(k_tail))

    @pl.when(k == nk - 1)
    def _():
        gamma = g_ref[...]
        beta = b_ref[...]

        def rows(r, carry):
            sl = pl.ds(pl.multiple_of(r * LN_ROWS, LN_ROWS), LN_ROWS)
            y = DEEPNORM_ALPHA * res_ref[sl, :] + scale * o_ref[sl, :]
            mu = jnp.mean(y, axis=-1, keepdims=True)
            dlt = y - mu
            var = jnp.mean(dlt * dlt, axis=-1, keepdims=True)
            out = dlt * lax.rsqrt(var + LN_EPS) * gamma + beta
            o_ref[sl, :] = out
            ob_ref[sl, :] = out.astype(ob_ref.dtype)
            return carry

        lax.fori_loop(0, o_ref.shape[0] // LN_ROWS, rows, 0)


def _mm_res_ln(a, w, layer, res, gamma, beta, scale):
    m, kp = a.shape
    _, k_valid, d = w.shape
    tm, tk = _tile(m, 512), _tile(kp, 1024)
    nk = kp // tk
    assert (nk - 1) * tk < k_valid <= kp
    return pl.pallas_call(
        functools.partial(_mm_res_ln_kernel, scale=scale, nk=nk, k_tail=k_valid - (nk - 1) * tk),
        grid=(m // tm, nk),
        in_specs=[pl.BlockSpec((tm, tk), lambda i, k: (i, k)),
                  pl.BlockSpec((1, tk, d), lambda i, k: (layer, k, 0)),
                  pl.BlockSpec((tm, d), lambda i, k: (i, 0), pipeline_mode=pl.Buffered(1)),
                  pl.BlockSpec((1, d), lambda i, k: (0, 0)),
                  pl.BlockSpec((1, d), lambda i, k: (0, 0))],
        out_specs=[pl.BlockSpec((tm, d), lambda i, k: (i, 0)),
                   pl.BlockSpec((tm, d), lambda i, k: (i, 0))],
        out_shape=[jax.ShapeDtypeStruct((m, d), F32), jax.ShapeDtypeStruct((m, d), BF16)],
        compiler_params=_cparams(("parallel", "arbitrary"), 56),
        name="mm_res_ln",
    )(a, w, res, gamma.reshape(1, d), beta.reshape(1, d))


def _proj_kernel(x_ref, w_ref, rope_ref, o_ref, *, n_query_tiles, n_rope_tiles):
    j = pl.program_id(1)
    acc = jnp.dot(x_ref[...], w_ref[...], preferred_element_type=F32)
    tn = acc.shape[1]

    def rope(table):
        cc = rope_ref[table, :, 0:HEAD_DIM]
        ss = rope_ref[table, :, HEAD_DIM:2 * HEAD_DIM]
        for c in range(tn // HEAD_DIM):
            head = acc[:, c * HEAD_DIM:(c + 1) * HEAD_DIM]
            rot = head * cc + pltpu.roll(head, HEAD_DIM // 2, 1) * ss
            o_ref[:, c * HEAD_DIM:(c + 1) * HEAD_DIM] = rot.astype(o_ref.dtype)

    pl.when(j < n_query_tiles)(lambda: rope(1))
    pl.when((j >= n_query_tiles) & (j < n_rope_tiles))(lambda: rope(0))

    @pl.when(j >= n_rope_tiles)
    def _():
        o_ref[...] = acc.astype(o_ref.dtype)


def _proj(xb, w, rope_tables, n_query_chunks, n_rope_chunks, out_dtype):
    m, d = xb.shape
    n = w.shape[1]
    tm, tn = _tile(m, 1024), PROJ_TN
    assert n % tn == 0 and (n_rope_chunks * HEAD_DIM) % tn == 0 and (n_query_chunks * HEAD_DIM) % tn == 0
    return pl.pallas_call(
        functools.partial(_proj_kernel, n_query_tiles=n_query_chunks * HEAD_DIM // tn,
                          n_rope_tiles=n_rope_chunks * HEAD_DIM // tn),
        grid=(m // tm, n // tn),
        in_specs=[pl.BlockSpec((tm, d), lambda i, j: (i, 0)),
                  pl.BlockSpec((d, tn), lambda i, j: (0, j)),
                  pl.BlockSpec((2, tm, 2 * HEAD_DIM), lambda i, j: (0, i, 0))],
        out_specs=pl.BlockSpec((tm, tn), lambda i, j: (i, j)),
        out_shape=jax.ShapeDtypeStruct((m, n), out_dtype),
        compiler_params=_cparams(("parallel", "arbitrary"), 40),
        name="in_proj",
    )(xb, w, rope_tables)


CONV_HALO = 32
CONV_TS = 128


def _conv_kernel(v_ref, g_ref, vh_ref, gh_ref, w_ref, b_ref, lg_ref, lb_ref, o_ref, ext_ref, shift_ref):
    i = pl.program_id(1)
    ts = v_ref.shape[1]
    halo = vh_ref[0] * jax.nn.sigmoid(gh_ref[0])
    ext_ref[0:CONV_HALO, :] = jnp.where(i > 0, halo, 0.0)
    ext_ref[CONV_HALO:, :] = v_ref[0] * jax.nn.sigmoid(g_ref[0])
    rows = shift_ref.shape[1]
    for r in range(1, SUBLANES):
        shift_ref[r - 1] = ext_ref[r:r + rows, :]
    first = CONV_HALO - (CONV_WIDTH - 1)
    cols = []
    for c in range(CONV_CH // LANES):
        lanes = slice(c * LANES, (c + 1) * LANES)
        acc = jnp.zeros((ts, LANES), F32) + b_ref[:, lanes]
        for k in range(CONV_WIDTH):
            base, r = (first + k) // SUBLANES * SUBLANES, (first + k) % SUBLANES
            window = ext_ref[base:base + ts, lanes] if r == 0 else shift_ref[r - 1, base:base + ts, lanes]
            acc = acc + w_ref[k:k + 1, lanes] * window
        cols.append(acc)
    y = jnp.concatenate(cols, axis=1)
    mu = jnp.mean(y, axis=-1, keepdims=True)
    dlt = y - mu
    var = jnp.mean(dlt * dlt, axis=-1, keepdims=True)
    z = dlt * lax.rsqrt(var + LN_EPS) * lg_ref[...] + lb_ref[...]
    o_ref[0] = (z * jax.nn.sigmoid(z)).astype(o_ref.dtype)


def _conv_module(hconv, dw_w, dw_b, ln_g, ln_b):
    b, s, _ = hconv.shape
    ts = _tile(s, CONV_TS)
    per = ts // CONV_HALO
    w_pad = jnp.concatenate([dw_w, jnp.zeros((CONV_HALO - CONV_WIDTH, CONV_CH), F32)], axis=0)
    vec = lambda a: a.reshape(1, CONV_CH)
    const = lambda shape: pl.BlockSpec(shape, lambda bi, i: (0, 0))
    return pl.pallas_call(
        _conv_kernel,
        grid=(b, s // ts),
        in_specs=[pl.BlockSpec((1, ts, CONV_CH), lambda bi, i: (bi, i, 0)),
                  pl.BlockSpec((1, ts, CONV_CH), lambda bi, i: (bi, i, 1)),
                  pl.BlockSpec((1, CONV_HALO, CONV_CH), lambda bi, i: (bi, jnp.maximum(i * per - 1, 0), 0)),
                  pl.BlockSpec((1, CONV_HALO, CONV_CH), lambda bi, i: (bi, jnp.maximum(i * per - 1, 0), 1)),
                  const((CONV_HALO, CONV_CH)), const((1, CONV_CH)), const((1, CONV_CH)), const((1, CONV_CH))],
        out_specs=pl.BlockSpec((1, ts, CONV_CH), lambda bi, i: (bi, i, 0)),
        out_shape=jax.ShapeDtypeStruct((b, s, CONV_CH), BF16),
        scratch_shapes=[pltpu.VMEM((ts + CONV_HALO, CONV_CH), F32),
                        pltpu.VMEM((SUBLANES - 1, ts + CONV_HALO - SUBLANES, CONV_CH), F32)],
        compiler_params=_cparams(("parallel", "arbitrary"), 32),
        name="conv_module",
    )(hconv, hconv, hconv, hconv, w_pad, vec(dw_b), vec(ln_g), vec(ln_b))


def _compress_kernel(t_ref, pos_ref, w1_ref, w2_ref, o_ref):
    t = t_ref[0, 0, 0]
    n, half = t.shape
    lo = jnp.dot((t + pos_ref[0, 0:1, :]).astype(BF16), w1_ref[0, 0:half, :], preferred_element_type=F32)
    hi = jnp.dot((t + pos_ref[0, 1:2, :]).astype(BF16), w1_ref[0, half:2 * half, :], preferred_element_type=F32)
    h1 = lo + pltpu.roll(hi, n - 1, 0)
    h1 = h1 * jax.nn.sigmoid(h1)
    out = jnp.dot(h1.astype(BF16), w2_ref[0], preferred_element_type=F32)
    row = lax.broadcasted_iota(jnp.int32, out.shape, 0)
    o_ref[0, 0, 0] = jnp.where(row < n - 1, out, 0.0).astype(o_ref.dtype)


def _compress(t4, pos, w1, w2):
    b, _, hkv, n, wide = t4.shape
    return pl.pallas_call(
        _compress_kernel,
        grid=(b, 2, hkv),
        in_specs=[pl.BlockSpec((1, 1, 1, n, wide), lambda bi, kv, h: (bi, kv, h, 0, 0)),
                  pl.BlockSpec((1, 2, wide), lambda bi, kv, h: (kv, 0, 0)),
                  pl.BlockSpec((1, 2 * wide, HEAD_DIM), lambda bi, kv, h: (kv, 0, 0)),
                  pl.BlockSpec((1, HEAD_DIM, HEAD_DIM), lambda bi, kv, h: (kv, 0, 0))],
        out_specs=pl.BlockSpec((1, 1, 1, n, HEAD_DIM), lambda bi, kv, h: (bi, kv, h, 0, 0)),
        out_shape=jax.ShapeDtypeStruct((b, 2, hkv, n, HEAD_DIM), BF16),
        compiler_params=_cparams(("parallel", "parallel", "arbitrary"), 40),
        name="nsa_compress",
    )(t4, pos, w1, w2)


_NT = (((1,), (1,)), ((), ()))


def _stack_heads(q, group):
    return jnp.concatenate([q[:, g * HEAD_DIM:(g + 1) * HEAD_DIM] for g in range(group)], axis=0)


def _online_step(qs, k, v, bias, carry):
    m, l, acc = carry
    s = lax.dot_general(qs, k, _NT, preferred_element_type=F32)
    if bias is not None:
        tq = bias.shape[0]
        s = jnp.concatenate([s[r:r + tq] + bias for r in range(0, s.shape[0], tq)], axis=0)
    m_new = jnp.maximum(m, jnp.max(s, axis=-1, keepdims=True))
    alpha = jnp.exp2(m - m_new)
    p = jnp.exp2(s - m_new)
    l_new = alpha * l + jnp.sum(p, axis=-1, keepdims=True)
    acc_new = alpha * acc + jnp.dot(p.astype(BF16), v, preferred_element_type=F32)
    return m_new, l_new, acc_new


def _online_init(rows):
    return (jnp.full((rows, 1), NEG_INF, F32), jnp.zeros((rows, 1), F32), jnp.zeros((rows, HEAD_DIM), F32))


NSA_TQ = 256
NSA_TK = 512


def _nsa_kernel(q_ref, ksl_ref, vsl_ref, kw_ref, vw_ref, kc_ref, vc_ref, gl_ref, gb_ref, ex_ref, o_ref, *, tk_sel):
    qi = pl.program_id(2)
    tq = q_ref.shape[1]
    seq = ksl_ref.shape[1]
    n_cmp = kc_ref.shape[3]
    n_sel = seq // SEL_BLOCK
    grp = NSA_GROUP
    t0 = qi * tq
    qs = _stack_heads(q_ref[0], grp)
    tpos = t0 + lax.broadcasted_iota(jnp.int32, (tq, 1), 0)

    cmp_end = lax.broadcasted_iota(jnp.int32, (1, n_cmp), 1) * CMP_STRIDE + (CMP_BLOCK - 1)
    valid_c = cmp_end <= tpos
    s_c = lax.dot_general(qs, kc_ref[0, 0, 0], _NT, preferred_element_type=F32)
    p_parts, p_sum = [], None
    for g in range(grp):
        sm = jnp.where(valid_c, s_c[g * tq:(g + 1) * tq], NEG_INF)
        p = jnp.where(valid_c, jnp.exp2(sm - jnp.max(sm, axis=-1, keepdims=True)), 0.0)
        den = jnp.sum(p, axis=-1, keepdims=True)
        p_c = p / jnp.where(den > 0.0, den, 1.0)
        p_parts.append(p_c.astype(BF16))
        p_sum = p_c if p_sum is None else p_sum + p_c
    o_c = jnp.dot(jnp.concatenate(p_parts, axis=0), vc_ref[0, 0, 0], preferred_element_type=F32)

    n_i = lax.broadcasted_iota(jnp.int32, (n_cmp, n_sel), 0)
    j_i = lax.broadcasted_iota(jnp.int32, (n_cmp, n_sel), 1)
    ratio = SEL_BLOCK // CMP_STRIDE
    overlap = jnp.where((n_i >= ratio * j_i - (CMP_BLOCK // CMP_STRIDE - 1)) & (n_i <= ratio * j_i + ratio - 1),
                        1.0, 0.0).astype(BF16)
    p_hi = p_sum.astype(BF16)
    p_lo = (p_sum - p_hi.astype(F32)).astype(BF16)
    imp = (jnp.dot(p_hi, overlap, preferred_element_type=F32) + jnp.dot(p_lo, overlap, preferred_element_type=F32))
    blk = lax.broadcasted_iota(jnp.int32, (1, n_sel), 1)
    t_blk = jnp.right_shift(tpos, SEL_SHIFT)
    forced = (blk == 0) | (blk == t_blk) | (blk == t_blk - 1)
    imp = jnp.where(forced, FORCE_SCORE, imp)
    imp = jnp.where(blk <= t_blk, imp, NEG_INF)

    imp_t = imp.T
    rowf = lax.broadcasted_iota(jnp.int32, (n_sel, tq), 0).astype(F32)
    sel_t = jnp.zeros((n_sel, tq), F32)
    for _ in range(min(SEL_TOPK, n_sel)):
        mx = jnp.max(imp_t, axis=0, keepdims=True)
        idx = jnp.min(jnp.where(imp_t == mx, rowf, float(n_sel)), axis=0, keepdims=True)
        pick = rowf == idx
        sel_t = jnp.where(pick, 1.0, sel_t)
        imp_t = jnp.where(pick, PICKED, imp_t)
    not_sel = (1.0 - sel_t.T).astype(BF16)

    qa = jnp.concatenate([qs, jnp.concatenate([not_sel] * grp, axis=0)], axis=1)

    def sel_step(kt, carry, causal):
        k0 = pl.multiple_of(kt * tk_sel, tk_sel)
        ka = jnp.concatenate([ksl_ref[0, pl.ds(k0, tk_sel), :], ex_ref[kt]], axis=1)
        bias = None
        if causal:
            kpos = k0 + lax.broadcasted_iota(jnp.int32, (1, tk_sel), 1)
            bias = jnp.where(kpos <= tpos, 0.0, NEG_INF)
        return _online_step(qa, ka, vsl_ref[0, pl.ds(k0, tk_sel), :], bias, carry)

    n_full = t0 // tk_sel
    full_step = functools.partial(sel_step, causal=False)
    n_pairs = n_full // 2
    sel_out = lax.fori_loop(0, n_pairs, lambda i, c: full_step(2 * i + 1, full_step(2 * i, c)),
                            _online_init(grp * tq))
    sel_out = lax.fori_loop(2 * n_pairs, n_full, full_step, sel_out)
    sel_out = sel_step(n_full, sel_out, causal=True)
    o_s = sel_out[2] / sel_out[1]

    span = NSA_WINDOW + tq
    k0 = pl.multiple_of(jnp.maximum(t0 - NSA_WINDOW, 0), tq)
    kpos = k0 + lax.broadcasted_iota(jnp.int32, (1, span), 1)
    bias = jnp.where((kpos <= tpos) & (kpos > tpos - NSA_WINDOW), 0.0, NEG_INF)
    win_out = _online_step(qs, kw_ref[0, pl.ds(k0, span), :], vw_ref[0, pl.ds(k0, span), :], bias,
                           _online_init(grp * tq))

    gate = jax.nn.sigmoid(gl_ref[0] + gb_ref[...])
    o_w = win_out[2] / win_out[1]
    for g in range(grp):
        rows = slice(g * tq, (g + 1) * tq)
        o = (gate[:, 3 * g:3 * g + 1] * o_c[rows] + gate[:, 3 * g + 1:3 * g + 2] * o_s[rows]
             + gate[:, 3 * g + 2:3 * g + 3] * o_w[rows])
        o_ref[0, :, g * HEAD_DIM:(g + 1) * HEAD_DIM] = o.astype(o_ref.dtype)


def _nsa(qkv, kvcmp, hconv, gate_b):
    b, s, _ = qkv.shape
    n = kvcmp.shape[3]
    tq = _tile(s, NSA_TQ)
    tk_sel = _tile(s, NSA_TK)
    assert NSA_WINDOW % tq == 0 and NSA_WINDOW + tq <= s and tk_sel % SEL_BLOCK == 0
    qw = NSA_GROUP * HEAD_DIM
    n_sel = s // SEL_BLOCK
    assert tk_sel % tq == 0
    key_block = np.arange(s).reshape(s // tk_sel, tk_sel, 1) // SEL_BLOCK
    onehot = key_block == np.arange(n_sel).reshape(1, 1, n_sel)
    expand = jnp.asarray(np.where(onehot, NEG_INF, 0.0), BF16)
    full = lambda chunk: pl.BlockSpec((1, s, HEAD_DIM), lambda bi, h, qi: (bi, 0, chunk + h))
    return pl.pallas_call(
        functools.partial(_nsa_kernel, tk_sel=tk_sel),
        grid=(b, NSA_KV_HEADS, s // tq),
        in_specs=[pl.BlockSpec((1, tq, qw), lambda bi, h, qi: (bi, qi, QKV_NSA_Q // NSA_GROUP + h)),
                  full(QKV_KSL), full(QKV_VSL), full(QKV_KWIN), full(QKV_VWIN),
                  pl.BlockSpec((1, 1, 1, n, HEAD_DIM), lambda bi, h, qi: (bi, 0, h, 0, 0)),
                  pl.BlockSpec((1, 1, 1, n, HEAD_DIM), lambda bi, h, qi: (bi, 1, h, 0, 0)),
                  pl.BlockSpec((1, tq, LANES), lambda bi, h, qi: (bi, qi, CONV_GATE_CHUNK + h)),
                  pl.BlockSpec((1, LANES), lambda bi, h, qi: (0, h)),
                  pl.BlockSpec(expand.shape, lambda bi, h, qi: (0, 0, 0))],
        out_specs=pl.BlockSpec((1, tq, qw), lambda bi, h, qi: (bi, qi, h)),
        out_shape=jax.ShapeDtypeStruct((b, s, NSA_HEADS * HEAD_DIM), BF16),
        compiler_params=_cparams(("parallel", "parallel", "arbitrary"), 48),
        name="nsa_attention",
    )(qkv, qkv, qkv, qkv, qkv, kvcmp, kvcmp, hconv, gate_b, expand)


SWA_TQ = 256


def _swa_kernel(sink_ref, q_ref, kp_ref, kc_ref, vp_ref, vc_ref, o_ref):
    qi = pl.program_id(1)
    tq = q_ref.shape[1]
    grp = SWA_GROUP
    t0 = qi * tq
    tpos = t0 + lax.broadcasted_iota(jnp.int32, (tq, 1), 0)
    kpos = t0 - SWA_WINDOW + lax.broadcasted_iota(jnp.int32, (1, SWA_WINDOW + tq), 1)
    bias = jnp.where((kpos >= 0) & (kpos <= tpos) & (kpos > tpos - SWA_WINDOW), 0.0, NEG_INF)
    for h in range(SWA_KV_HEADS):
        kv = slice(h * HEAD_DIM, (h + 1) * HEAD_DIM)
        qs = _stack_heads(q_ref[0, :, h * grp * HEAD_DIM:(h + 1) * grp * HEAD_DIM], grp)
        k = jnp.concatenate([kp_ref[0, :, kv], kc_ref[0, :, kv]], axis=0)
        v = jnp.concatenate([vp_ref[0, :, kv], vc_ref[0, :, kv]], axis=0)
        s = lax.dot_general(qs, k, _NT, preferred_element_type=F32)
        s = jnp.concatenate([s[g * tq:(g + 1) * tq] + bias for g in range(grp)], axis=0)
        sink = jnp.concatenate([jnp.full((tq, 1), sink_ref[h * grp + g] * LOG2E, F32) for g in range(grp)], axis=0)
        m = jnp.maximum(jnp.max(s, axis=-1, keepdims=True), sink)
        p = jnp.exp2(s - m)
        den = jnp.sum(p, axis=-1, keepdims=True) + jnp.exp2(sink - m)
        o = jnp.dot(p.astype(BF16), v, preferred_element_type=F32) / den
        for g in range(grp):
            col = (h * grp + g) * HEAD_DIM
            o_ref[0, :, col:col + HEAD_DIM] = o[g * tq:(g + 1) * tq].astype(o_ref.dtype)


def _swa(qkv, sinks):
    b, s, _ = qkv.shape
    tq = _tile(s, SWA_TQ)
    per = tq // SWA_WINDOW
    qw = SWA_HEADS * HEAD_DIM
    kw = SWA_KV_HEADS * HEAD_DIM
    assert QKV_SWA_Q == 0 and QKV_SWA_K % SWA_KV_HEADS == 0 and QKV_SWA_V % SWA_KV_HEADS == 0
    prev = lambda chunk: pl.BlockSpec((1, SWA_WINDOW, kw),
                                      lambda bi, qi: (bi, jnp.maximum(qi * per - 1, 0), chunk // SWA_KV_HEADS))
    cur = lambda chunk: pl.BlockSpec((1, tq, kw), lambda bi, qi: (bi, qi, chunk // SWA_KV_HEADS))
    return pl.pallas_call(
        _swa_kernel,
        grid=(b, s // tq),
        in_specs=[pl.BlockSpec(memory_space=pltpu.SMEM),
                  pl.BlockSpec((1, tq, qw), lambda bi, qi: (bi, qi, 0)),
                  prev(QKV_SWA_K), cur(QKV_SWA_K), prev(QKV_SWA_V), cur(QKV_SWA_V)],
        out_specs=pl.BlockSpec((1, tq, qw), lambda bi, qi: (bi, qi, 0)),
        out_shape=jax.ShapeDtypeStruct((b, s, qw), BF16),
        compiler_params=_cparams(("parallel", "arbitrary"), 32),
        name="swa_attention",
    )(sinks, qkv, qkv, qkv, qkv, qkv)


def _split_in_proj(w_in, gate_b):
    sizes = (CONV_CH, CONV_CH, NSA_HEADS * HEAD_DIM) + (NSA_KV_HEADS * HEAD_DIM,) * 6 + \
            (NSA_HEADS * 3, SWA_HEADS * HEAD_DIM, SWA_KV_HEADS * HEAD_DIM, SWA_KV_HEADS * HEAD_DIM)
    off = np.concatenate([[0], np.cumsum(sizes)])
    seg = lambda i: w_in[:, off[i]:off[i + 1]]
    a_val, a_gate, q_b, kc, vc, ksl, vsl, kw, vw, g_b, q_c, k_c, v_c = (seg(i) for i in range(13))
    gap = jnp.zeros((w_in.shape[0], 2 * HEAD_DIM), F32)
    w_qkv = jnp.concatenate([q_c, q_b, ksl, kw, k_c, gap, vsl, vw, v_c, gap], axis=1).astype(BF16)
    assert w_qkv.shape[1] == QKV_CHUNKS * HEAD_DIM
    w_cmp = jnp.concatenate([kc, vc], axis=1).astype(BF16)
    per_head = NSA_GROUP * 3
    pad_w = jnp.zeros((w_in.shape[0], LANES - per_head), F32)
    pad_b = jnp.zeros((LANES - per_head,), F32)
    gate_w, gate_bias = [], []
    for h in range(NSA_KV_HEADS):
        gate_w += [g_b[:, h * per_head:(h + 1) * per_head], pad_w]
        gate_bias += [gate_b[h * per_head:(h + 1) * per_head], pad_b]
    w_conv = jnp.concatenate([a_val, a_gate] + gate_w, axis=1).astype(BF16)
    return w_qkv, w_cmp, w_conv, jnp.concatenate(gate_bias).reshape(1, NSA_KV_HEADS * LANES)


def _ffn_block(x2, xb, w_gu, w_down, layer, ln_g, ln_b):
    hidden = _ffn_up(xb, w_gu, layer, w_down.shape[1])
    return _mm_res_ln(hidden, w_down, layer, x2, ln_g, ln_b, 0.5)


def _mixer_block(x2, xb, rope, batch, w_in, w_out, layer, dw_w, dw_b, cln_g, cln_b, cmp_pos, cmp_w1, cmp_w2,
                 gate_b, sinks, ln_g, ln_b):
    m = x2.shape[0]
    s = m // batch
    w_qkv, w_cmp, w_conv, gate_bias = _split_in_proj(w_in, gate_b)
    qkv = _proj(xb, w_qkv, rope, QKV_QUERY_CHUNKS, QKV_ROPE_CHUNKS, BF16).reshape(batch, s, -1)
    hcmp = _proj(xb, w_cmp, rope, 0, NSA_KV_HEADS, F32)
    hconv = _proj(xb, w_conv, rope, 0, 0, F32).reshape(batch, s, -1)
    out_a = _conv_module(hconv, dw_w, dw_b, cln_g, cln_b)
    t4 = hcmp.reshape(batch, s // CMP_STRIDE, CMP_STRIDE, 2, NSA_KV_HEADS, HEAD_DIM)
    t4 = t4.transpose(0, 3, 4, 1, 2, 5).reshape(batch, 2, NSA_KV_HEADS, s // CMP_STRIDE, CMP_STRIDE * HEAD_DIM)
    pos = cmp_pos.reshape(2, 2, CMP_STRIDE * HEAD_DIM)
    kvcmp = _compress(t4, pos, cmp_w1.astype(BF16), cmp_w2.astype(BF16))
    out_b = _nsa(qkv, kvcmp, hconv, gate_bias)
    out_c = _swa(qkv, sinks)
    mixed = jnp.concatenate([out_a, out_b, out_c], axis=-1).reshape(m, -1)
    return _mm_res_ln(mixed, w_out, layer, x2, ln_g, ln_b, 1.0)


def _rope_tables(positions):
    inv = jnp.power(ROPE_THETA, -jnp.arange(0, HEAD_DIM, 2, dtype=F32) / HEAD_DIM)
    ang = positions.astype(F32).reshape(-1, 1) * inv
    cos, sin = jnp.cos(ang), jnp.sin(ang)
    table = jnp.concatenate([cos, cos, -sin, sin], axis=-1)
    return jnp.stack([table, table * QUERY_SCALE])


def kernel(x, positions, ffn1_w_gu, ffn1_w_down, ln1_g, ln1_b, w_in, conv_dw_w, conv_dw_b, conv_ln_g, conv_ln_b,
           nsa_cmp_pos, nsa_cmp_w1, nsa_cmp_w2, nsa_gate_b, swa_sinks, w_out, ln2_g, ln2_b,
           ffn2_w_gu, ffn2_w_down, ln3_g, ln3_b):
    batch, s, d = x.shape
    rope = _rope_tables(positions)
    x2 = x.reshape(batch * s, d)
    xb = x2.astype(BF16)
    gu1, down1, gu2, down2, out_w = (w.astype(BF16) for w in (ffn1_w_gu, ffn1_w_down, ffn2_w_gu, ffn2_w_down, w_out))
    for i in range(w_in.shape[0]):
        x2, xb = _ffn_block(x2, xb, gu1, down1, i, ln1_g[i], ln1_b[i])
        x2, xb = _mixer_block(x2, xb, rope, batch, w_in[i], out_w, i, conv_dw_w[i], conv_dw_b[i], conv_ln_g[i],
                              conv_ln_b[i], nsa_cmp_pos[i], nsa_cmp_w1[i], nsa_cmp_w2[i], nsa_gate_b[i],
                              swa_sinks[i], ln2_g[i], ln2_b[i])
        x2, xb = _ffn_block(x2, xb, gu2, down2, i, ln3_g[i], ln3_b[i])
    return x2.reshape(batch, s, d)
```

```python
import functools

import jax
import jax.numpy as jnp
import numpy as np
from jax import lax
from jax.experimental import pallas as pl
from jax.experimental.pallas import tpu as pltpu

F32 = jnp.float32
BF16 = jnp.bfloat16

HEAD_DIM = 128
CONV_CH = 1024
CONV_WIDTH = 31
NSA_HEADS = 12
NSA_KV_HEADS = 4
NSA_GROUP = NSA_HEADS // NSA_KV_HEADS
SWA_HEADS = 12
SWA_KV_HEADS = 2
SWA_GROUP = SWA_HEADS // SWA_KV_HEADS
CMP_BLOCK = 32
CMP_STRIDE = 16
SEL_BLOCK = 64
SEL_SHIFT = 6
SEL_TOPK = 16
NSA_WINDOW = 512
SWA_WINDOW = 128
ROPE_THETA = 10000.0
LN_EPS = 1e-5
DEPTH = 2
DEEPNORM_ALPHA = (2 * DEPTH) ** 0.25
NEG_INF = -1e30
FORCE_SCORE = 1e4
PICKED = -3e38
ATTN_SCALE = HEAD_DIM ** -0.5
LOG2E = 1.4426950408889634
QUERY_SCALE = ATTN_SCALE * LOG2E

V7X_VMEM_BYTES = 64 * 1024 * 1024
LANES = 128
SUBLANES = 8

QKV_SWA_Q = 0
QKV_NSA_Q = 12
QKV_QUERY_CHUNKS = 24
QKV_KSL = 24
QKV_KWIN = 28
QKV_SWA_K = 32
QKV_ROPE_CHUNKS = 36
QKV_VSL = 36
QKV_VWIN = 40
QKV_SWA_V = 44
QKV_CHUNKS = 48
PROJ_TN = 512
CONV_GATE_CHUNK = 2 * CONV_CH // LANES


def _cparams(semantics, vmem_mib):
    return pltpu.CompilerParams(dimension_semantics=semantics, vmem_limit_bytes=vmem_mib * 1024 * 1024)


def _tile(n, pref):
    t = min(n, pref)
    while n % t:
        t //= 2
    return t


FFN_WBLK = 256
FFN_HIDDEN_ALIGN = 1024


def _gu_kernel(x_ref, wg0_ref, wg1_ref, wu0_ref, wu1_ref, o_ref, *, d_ff):
    j = pl.program_id(1)
    x = x_ref[...]
    wb = wg0_ref.shape[2]
    for half, (wg_ref, wu_ref) in enumerate(((wg0_ref, wu0_ref), (wg1_ref, wu1_ref))):
        g = jnp.dot(x, wg_ref[0], preferred_element_type=F32)
        u = jnp.dot(x, wu_ref[0], preferred_element_type=F32)
        h = g * jax.nn.sigmoid(g) * u
        col = (2 * j + half) * wb + lax.broadcasted_iota(jnp.int32, (1, wb), 1)
        o_ref[:, half * wb:(half + 1) * wb] = jnp.where(col < d_ff, h, 0.0).astype(o_ref.dtype)


def _ffn_up(xb, w_gu, layer, d_ff):
    m, d = xb.shape
    wb = FFN_WBLK
    assert d_ff % wb == 0
    fp = -(-d_ff // FFN_HIDDEN_ALIGN) * FFN_HIDDEN_ALIGN
    up0 = d_ff // wb
    last = 2 * d_ff // wb - 1
    tm = _tile(m, 1024)
    wspec = lambda f: pl.BlockSpec((1, d, wb), lambda i, j: (layer, 0, jnp.minimum(f(j), last)))
    return pl.pallas_call(
        functools.partial(_gu_kernel, d_ff=d_ff),
        grid=(m // tm, fp // (2 * wb)),
        in_specs=[pl.BlockSpec((tm, d), lambda i, j: (i, 0)),
                  wspec(lambda j: 2 * j), wspec(lambda j: 2 * j + 1),
                  wspec(lambda j: up0 + 2 * j), wspec(lambda j: up0 + 2 * j + 1)],
        out_specs=pl.BlockSpec((tm, 2 * wb), lambda i, j: (i, j)),
        out_shape=jax.ShapeDtypeStruct((m, fp), BF16),
        compiler_params=_cparams(("parallel", "arbitrary"), 48),
        name="ffn_up",
    )(xb, w_gu, w_gu, w_gu, w_gu)


LN_ROWS = 32
LN_CHUNK = 128
MM_COLS = 1024


def _mm_res_ln_kernel(a_ref, w_ref, res_hbm, g_ref, b_ref, of_hbm, ob_hbm,
                      acc_ref, res_buf, of_buf, ob_buf, res_sem, of_sem, ob_sem, *, scale, nk, k_tail):
    i = pl.program_id(0)
    k = pl.program_id(1)
    tm, tk = a_ref.shape
    n_chunks = tm // LN_CHUNK
    row0 = i * tm

    def rows_of(c):
        return pl.ds(pl.multiple_of(row0 + c * LN_CHUNK, LN_CHUNK), LN_CHUNK)

    def res_copy(c):
        return pltpu.make_async_copy(res_hbm.at[rows_of(c), :], res_buf.at[c % 2], res_sem.at[c % 2])

    def of_copy(c):
        return pltpu.make_async_copy(of_buf.at[c % 2], of_hbm.at[rows_of(c), :], of_sem.at[c % 2])

    def ob_copy(c):
        return pltpu.make_async_copy(ob_buf.at[c % 2], ob_hbm.at[rows_of(c), :], ob_sem.at[c % 2])

    @pl.when(k == 0)
    def _():
        acc_ref[...] = jnp.zeros_like(acc_ref)

    @pl.when(k == nk - 1)
    def _():
        for c in range(min(2, n_chunks)):
            res_copy(c).start()

    def accumulate(valid_rows):
        a = a_ref[...]
        for c in range(acc_ref.shape[1] // MM_COLS):
            cols = slice(c * MM_COLS, (c + 1) * MM_COLS)
            w = w_ref[0, :, cols]
            if valid_rows < tk:
                w = jnp.where(lax.broadcasted_iota(jnp.int32, w.shape, 0) < valid_rows, w, jnp.zeros_like(w))
            acc_ref[:, cols] += jnp.dot(a, w, preferred_element_type=F32)

    if k_tail == tk:
        accumulate(tk)
    else:
        pl.when(k < nk - 1)(lambda: accumulate(tk))
        pl.when(k == nk - 1)(lambda: accumulate(k_tail))

    @pl.when(k == nk - 1)
    def _():
        gamma = g_ref[...]
        beta = b_ref[...]
        for c in range(n_chunks):
            slot = c % 2
            res_copy(c).wait()
            if c >= 2:
                of_copy(c - 2).wait()
                ob_copy(c - 2).wait()

            def rows(r, carry):
                sl = pl.ds(pl.multiple_of(r * LN_ROWS, LN_ROWS), LN_ROWS)
                src = pl.ds(pl.multiple_of(c * LN_CHUNK + r * LN_ROWS, LN_ROWS), LN_ROWS)
                y = DEEPNORM_ALPHA * res_buf[slot, sl, :] + scale * acc_ref[src, :]
                mu = jnp.mean(y, axis=-1, keepdims=True)
                dlt = y - mu
                var = jnp.mean(dlt * dlt, axis=-1, keepdims=True)
                out = dlt * lax.rsqrt(var + LN_EPS) * gamma + beta
                of_buf[slot, sl, :] = out
                ob_buf[slot, sl, :] = out.astype(ob_buf.dtype)
                return carry

            lax.fori_loop(0, LN_CHUNK // LN_ROWS, rows, 0)
            of_copy(c).start()
            ob_copy(c).start()
            if c + 2 < n_chunks:
                res_copy(c + 2).start()
        for c in range(max(n_chunks - 2, 0), n_chunks):
            of_copy(c).wait()
            ob_copy(c).wait()


def _mm_res_ln(a, w, layer, res, gamma, beta, scale):
    m, kp = a.shape
    _, k_valid, d = w.shape
    tm, tk = _tile(m, 1024), _tile(kp, 1024)
    nk = kp // tk
    assert (nk - 1) * tk < k_valid <= kp and tm % LN_CHUNK == 0 and d % MM_COLS == 0
    hbm = pl.BlockSpec(memory_space=pl.ANY)
    return pl.pallas_call(
        functools.partial(_mm_res_ln_kernel, scale=scale, nk=nk, k_tail=k_valid - (nk - 1) * tk),
        grid=(m // tm, nk),
        in_specs=[pl.BlockSpec((tm, tk), lambda i, k: (i, k)),
                  pl.BlockSpec((1, tk, d), lambda i, k: (layer, k, 0)),
                  hbm,
                  pl.BlockSpec((1, d), lambda i, k: (0, 0)),
                  pl.BlockSpec((1, d), lambda i, k: (0, 0))],
        out_specs=[hbm, hbm],
        out_shape=[jax.ShapeDtypeStruct((m, d), F32), jax.ShapeDtypeStruct((m, d), BF16)],
        scratch_shapes=[pltpu.VMEM((tm, d), F32),
                        pltpu.VMEM((2, LN_CHUNK, d), F32), pltpu.VMEM((2, LN_CHUNK, d), F32),
                        pltpu.VMEM((2, LN_CHUNK, d), BF16),
                        pltpu.SemaphoreType.DMA((2,)), pltpu.SemaphoreType.DMA((2,)), pltpu.SemaphoreType.DMA((2,))],
        compiler_params=_cparams(("arbitrary", "arbitrary"), 56),
        name="mm_res_ln",
    )(a, w, res, gamma.reshape(1, d), beta.reshape(1, d))


def _proj_kernel(x_ref, w_ref, rope_ref, o_ref, *, n_query_tiles, n_rope_tiles):
    j = pl.program_id(1)
    acc = jnp.dot(x_ref[...], w_ref[...], preferred_element_type=F32)
    tn = acc.shape[1]

    def rope(table):
        cc = rope_ref[table, :, 0:HEAD_DIM]
        ss = rope_ref[table, :, HEAD_DIM:2 * HEAD_DIM]
        for c in range(tn // HEAD_DIM):
            head = acc[:, c * HEAD_DIM:(c + 1) * HEAD_DIM]
            rot = head * cc + pltpu.roll(head, HEAD_DIM // 2, 1) * ss
            o_ref[:, c * HEAD_DIM:(c + 1) * HEAD_DIM] = rot.astype(o_ref.dtype)

    pl.when(j < n_query_tiles)(lambda: rope(1))
    pl.when((j >= n_query_tiles) & (j < n_rope_tiles))(lambda: rope(0))

    @pl.when(j >= n_rope_tiles)
    def _():
        o_ref[...] = acc.astype(o_ref.dtype)


def _proj(xb, w, rope_tables, n_query_chunks, n_rope_chunks, out_dtype):
    m, d = xb.shape
    n = w.shape[1]
    tm, tn = _tile(m, 1024), PROJ_TN
    assert n % tn == 0 and (n_rope_chunks * HEAD_DIM) % tn == 0 and (n_query_chunks * HEAD_DIM) % tn == 0
    return pl.pallas_call(
        functools.partial(_proj_kernel, n_query_tiles=n_query_chunks * HEAD_DIM // tn,
                          n_rope_tiles=n_rope_chunks * HEAD_DIM // tn),
        grid=(m // tm, n // tn),
        in_specs=[pl.BlockSpec((tm, d), lambda i, j: (i, 0)),
                  pl.BlockSpec((d, tn), lambda i, j: (0, j)),
                  pl.BlockSpec((2, tm, 2 * HEAD_DIM), lambda i, j: (0, i, 0))],
        out_specs=pl.BlockSpec((tm, tn), lambda i, j: (i, j)),
        out_shape=jax.ShapeDtypeStruct((m, n), out_dtype),
        compiler_params=_cparams(("parallel", "arbitrary"), 40),
        name="in_proj",
    )(xb, w, rope_tables)


CONV_HALO = 32
CONV_TS = 128


def _conv_kernel(v_ref, g_ref, vh_ref, gh_ref, w_ref, b_ref, lg_ref, lb_ref, o_ref, ext_ref, shift_ref):
    i = pl.program_id(1)
    ts = v_ref.shape[1]
    halo = vh_ref[0] * jax.nn.sigmoid(gh_ref[0])
    ext_ref[0:CONV_HALO, :] = jnp.where(i > 0, halo, 0.0)
    ext_ref[CONV_HALO:, :] = v_ref[0] * jax.nn.sigmoid(g_ref[0])
    rows = shift_ref.shape[1]
    for r in range(1, SUBLANES):
        shift_ref[r - 1] = ext_ref[r:r + rows, :]
    first = CONV_HALO - (CONV_WIDTH - 1)
    cols = []
    for c in range(CONV_CH // LANES):
        lanes = slice(c * LANES, (c + 1) * LANES)
        acc = jnp.zeros((ts, LANES), F32) + b_ref[:, lanes]
        for k in range(CONV_WIDTH):
            base, r = (first + k) // SUBLANES * SUBLANES, (first + k) % SUBLANES
            window = ext_ref[base:base + ts, lanes] if r == 0 else shift_ref[r - 1, base:base + ts, lanes]
            acc = acc + w_ref[k:k + 1, lanes] * window
        cols.append(acc)
    y = jnp.concatenate(cols, axis=1)
    mu = jnp.mean(y, axis=-1, keepdims=True)
    dlt = y - mu
    var = jnp.mean(dlt * dlt, axis=-1, keepdims=True)
    z = dlt * lax.rsqrt(var + LN_EPS) * lg_ref[...] + lb_ref[...]
    o_ref[0] = (z * jax.nn.sigmoid(z)).astype(o_ref.dtype)


def _conv_module(hconv, dw_w, dw_b, ln_g, ln_b):
    b, s, _ = hconv.shape
    ts = _tile(s, CONV_TS)
    per = ts // CONV_HALO
    w_pad = jnp.concatenate([dw_w, jnp.zeros((CONV_HALO - CONV_WIDTH, CONV_CH), F32)], axis=0)
    vec = lambda a: a.reshape(1, CONV_CH)
    const = lambda shape: pl.BlockSpec(shape, lambda bi, i: (0, 0))
    return pl.pallas_call(
        _conv_kernel,
        grid=(b, s // ts),
        in_specs=[pl.BlockSpec((1, ts, CONV_CH), lambda bi, i: (bi, i, 0)),
                  pl.BlockSpec((1, ts, CONV_CH), lambda bi, i: (bi, i, 1)),
                  pl.BlockSpec((1, CONV_HALO, CONV_CH), lambda bi, i: (bi, jnp.maximum(i * per - 1, 0), 0)),
                  pl.BlockSpec((1, CONV_HALO, CONV_CH), lambda bi, i: (bi, jnp.maximum(i * per - 1, 0), 1)),
                  const((CONV_HALO, CONV_CH)), const((1, CONV_CH)), const((1, CONV_CH)), const((1, CONV_CH))],
        out_specs=pl.BlockSpec((1, ts, CONV_CH), lambda bi, i: (bi, i, 0)),
        out_shape=jax.ShapeDtypeStruct((b, s, CONV_CH), BF16),
        scratch_shapes=[pltpu.VMEM((ts + CONV_HALO, CONV_CH), F32),
                        pltpu.VMEM((SUBLANES - 1, ts + CONV_HALO - SUBLANES, CONV_CH), F32)],
        compiler_params=_cparams(("parallel", "arbitrary"), 32),
        name="conv_module",
    )(hconv, hconv, hconv, hconv, w_pad, vec(dw_b), vec(ln_g), vec(ln_b))


def _compress_kernel(t_ref, pos_ref, w1_ref, w2_ref, o_ref):
    t = t_ref[0, 0, 0]
    n, half = t.shape
    lo = jnp.dot((t + pos_ref[0, 0:1, :]).astype(BF16), w1_ref[0, 0:half, :], preferred_element_type=F32)
    hi = jnp.dot((t + pos_ref[0, 1:2, :]).astype(BF16), w1_ref[0, half:2 * half, :], preferred_element_type=F32)
    h1 = lo + pltpu.roll(hi, n - 1, 0)
    h1 = h1 * jax.nn.sigmoid(h1)
    out = jnp.dot(h1.astype(BF16), w2_ref[0], preferred_element_type=F32)
    row = lax.broadcasted_iota(jnp.int32, out.shape, 0)
    o_ref[0, 0, 0] = jnp.where(row < n - 1, out, 0.0).astype(o_ref.dtype)


def _compress(t4, pos, w1, w2):
    b, _, hkv, n, wide = t4.shape
    return pl.pallas_call(
        _compress_kernel,
        grid=(b, 2, hkv),
        in_specs=[pl.BlockSpec((1, 1, 1, n, wide), lambda bi, kv, h: (bi, kv, h, 0, 0)),
                  pl.BlockSpec((1, 2, wide), lambda bi, kv, h: (kv, 0, 0)),
                  pl.BlockSpec((1, 2 * wide, HEAD_DIM), lambda bi, kv, h: (kv, 0, 0)),
                  pl.BlockSpec((1, HEAD_DIM, HEAD_DIM), lambda bi, kv, h: (kv, 0, 0))],
        out_specs=pl.BlockSpec((1, 1, 1, n, HEAD_DIM), lambda bi, kv, h: (bi, kv, h, 0, 0)),
        out_shape=jax.ShapeDtypeStruct((b, 2, hkv, n, HEAD_DIM), BF16),
        compiler_params=_cparams(("parallel", "parallel", "arbitrary"), 40),
        name="nsa_compress",
    )(t4, pos, w1, w2)


_NT = (((1,), (1,)), ((), ()))


def _stack_heads(q, group):
    return jnp.concatenate([q[:, g * HEAD_DIM:(g + 1) * HEAD_DIM] for g in range(group)], axis=0)


def _online_step(qs, k, v, bias, carry):
    m, l, acc = carry
    s = lax.dot_general(qs, k, _NT, preferred_element_type=F32)
    if bias is not None:
        tq = bias.shape[0]
        s = jnp.concatenate([s[r:r + tq] + bias for r in range(0, s.shape[0], tq)], axis=0)
    m_new = jnp.maximum(m, jnp.max(s, axis=-1, keepdims=True))
    alpha = jnp.exp2(m - m_new)
    p = jnp.exp2(s - m_new)
    l_new = alpha * l + jnp.sum(p, axis=-1, keepdims=True)
    acc_new = alpha * acc + jnp.dot(p.astype(BF16), v, preferred_element_type=F32)
    return m_new, l_new, acc_new


def _online_init(rows):
    return (jnp.full((rows, 1), NEG_INF, F32), jnp.zeros((rows, 1), F32), jnp.zeros((rows, HEAD_DIM), F32))


NSA_TQ = 256
NSA_TK = 512


def _nsa_kernel(q_ref, ksl_ref, vsl_ref, kw_ref, vw_ref, kc_ref, vc_ref, gl_ref, gb_ref, ex_ref, o_ref, *, tk_sel):
    qi = pl.program_id(2)
    tq = q_ref.shape[1]
    seq = ksl_ref.shape[1]
    n_cmp = kc_ref.shape[3]
    n_sel = seq // SEL_BLOCK
    grp = NSA_GROUP
    t0 = qi * tq
    qs = _stack_heads(q_ref[0], grp)
    tpos = t0 + lax.broadcasted_iota(jnp.int32, (tq, 1), 0)

    cmp_end = lax.broadcasted_iota(jnp.int32, (1, n_cmp), 1) * CMP_STRIDE + (CMP_BLOCK - 1)
    valid_c = cmp_end <= tpos
    s_c = lax.dot_general(qs, kc_ref[0, 0, 0], _NT, preferred_element_type=F32)
    p_parts, p_sum = [], None
    for g in range(grp):
        sm = jnp.where(valid_c, s_c[g * tq:(g + 1) * tq], NEG_INF)
        p = jnp.where(valid_c, jnp.exp2(sm - jnp.max(sm, axis=-1, keepdims=True)), 0.0)
        den = jnp.sum(p, axis=-1, keepdims=True)
        p_c = p / jnp.where(den > 0.0, den, 1.0)
        p_parts.append(p_c.astype(BF16))
        p_sum = p_c if p_sum is None else p_sum + p_c
    o_c = jnp.dot(jnp.concatenate(p_parts, axis=0), vc_ref[0, 0, 0], preferred_element_type=F32)

    n_i = lax.broadcasted_iota(jnp.int32, (n_cmp, n_sel), 0)
    j_i = lax.broadcasted_iota(jnp.int32, (n_cmp, n_sel), 1)
    ratio = SEL_BLOCK // CMP_STRIDE
    overlap = jnp.where((n_i >= ratio * j_i - (CMP_BLOCK // CMP_STRIDE - 1)) & (n_i <= ratio * j_i + ratio - 1),
                        1.0, 0.0).astype(BF16)
    p_hi = p_sum.astype(BF16)
    p_lo = (p_sum - p_hi.astype(F32)).astype(BF16)
    imp = (jnp.dot(p_hi, overlap, preferred_element_type=F32) + jnp.dot(p_lo, overlap, preferred_element_type=F32))
    blk = lax.broadcasted_iota(jnp.int32, (1, n_sel), 1)
    t_blk = jnp.right_shift(tpos, SEL_SHIFT)
    forced = (blk == 0) | (blk == t_blk) | (blk == t_blk - 1)
    imp = jnp.where(forced, FORCE_SCORE, imp)
    imp = jnp.where(blk <= t_blk, imp, NEG_INF)

    imp_t = imp.T
    rowf = lax.broadcasted_iota(jnp.int32, (n_sel, tq), 0).astype(F32)
    sel_t = jnp.zeros((n_sel, tq), F32)
    for _ in range(min(SEL_TOPK, n_sel)):
        mx = jnp.max(imp_t, axis=0, keepdims=True)
        idx = jnp.min(jnp.where(imp_t == mx, rowf, float(n_sel)), axis=0, keepdims=True)
        pick = rowf == idx
        sel_t = jnp.where(pick, 1.0, sel_t)
        imp_t = jnp.where(pick, PICKED, imp_t)
    not_sel = (1.0 - sel_t.T).astype(BF16)

    qa = jnp.concatenate([qs, jnp.concatenate([not_sel] * grp, axis=0)], axis=1)

    def sel_step(kt, carry, causal):
        k0 = pl.multiple_of(kt * tk_sel, tk_sel)
        ka = jnp.concatenate([ksl_ref[0, pl.ds(k0, tk_sel), :], ex_ref[kt]], axis=1)
        bias = None
        if causal:
            kpos = k0 + lax.broadcasted_iota(jnp.int32, (1, tk_sel), 1)
            bias = jnp.where(kpos <= tpos, 0.0, NEG_INF)
        return _online_step(qa, ka, vsl_ref[0, pl.ds(k0, tk_sel), :], bias, carry)

    n_full = t0 // tk_sel
    full_step = functools.partial(sel_step, causal=False)
    n_pairs = n_full // 2
    sel_out = lax.fori_loop(0, n_pairs, lambda i, c: full_step(2 * i + 1, full_step(2 * i, c)),
                            _online_init(grp * tq))
    sel_out = lax.fori_loop(2 * n_pairs, n_full, full_step, sel_out)
    sel_out = sel_step(n_full, sel_out, causal=True)
    o_s = sel_out[2] / sel_out[1]

    span = NSA_WINDOW + tq
    k0 = pl.multiple_of(jnp.maximum(t0 - NSA_WINDOW, 0), tq)
    kpos = k0 + lax.broadcasted_iota(jnp.int32, (1, span), 1)
    bias = jnp.where((kpos <= tpos) & (kpos > tpos - NSA_WINDOW), 0.0, NEG_INF)
    win_out = _online_step(qs, kw_ref[0, pl.ds(k0, span), :], vw_ref[0, pl.ds(k0, span), :], bias,
                           _online_init(grp * tq))

    gate = jax.nn.sigmoid(gl_ref[0] + gb_ref[...])
    o_w = win_out[2] / win_out[1]
    for g in range(grp):
        rows = slice(g * tq, (g + 1) * tq)
        o = (gate[:, 3 * g:3 * g + 1] * o_c[rows] + gate[:, 3 * g + 1:3 * g + 2] * o_s[rows]
             + gate[:, 3 * g + 2:3 * g + 3] * o_w[rows])
        o_ref[0, :, g * HEAD_DIM:(g + 1) * HEAD_DIM] = o.astype(o_ref.dtype)


def _nsa(qkv, kvcmp, hconv, gate_b):
    b, s, _ = qkv.shape
    n = kvcmp.shape[3]
    tq = _tile(s, NSA_TQ)
    tk_sel = _tile(s, NSA_TK)
    assert NSA_WINDOW % tq == 0 and NSA_WINDOW + tq <= s and tk_sel % SEL_BLOCK == 0
    qw = NSA_GROUP * HEAD_DIM
    n_sel = s // SEL_BLOCK
    assert tk_sel % tq == 0
    key_block = np.arange(s).reshape(s // tk_sel, tk_sel, 1) // SEL_BLOCK
    onehot = key_block == np.arange(n_sel).reshape(1, 1, n_sel)
    expand = jnp.asarray(np.where(onehot, NEG_INF, 0.0), BF16)
    full = lambda chunk: pl.BlockSpec((1, s, HEAD_DIM), lambda bi, h, qi: (bi, 0, chunk + h))
    return pl.pallas_call(
        functools.partial(_nsa_kernel, tk_sel=tk_sel),
        grid=(b, NSA_KV_HEADS, s // tq),
        in_specs=[pl.BlockSpec((1, tq, qw), lambda bi, h, qi: (bi, qi, QKV_NSA_Q // NSA_GROUP + h)),
                  full(QKV_KSL), full(QKV_VSL), full(QKV_KWIN), full(QKV_VWIN),
                  pl.BlockSpec((1, 1, 1, n, HEAD_DIM), lambda bi, h, qi: (bi, 0, h, 0, 0)),
                  pl.BlockSpec((1, 1, 1, n, HEAD_DIM), lambda bi, h, qi: (bi, 1, h, 0, 0)),
                  pl.BlockSpec((1, tq, LANES), lambda bi, h, qi: (bi, qi, CONV_GATE_CHUNK + h)),
                  pl.BlockSpec((1, LANES), lambda bi, h, qi: (0, h)),
                  pl.BlockSpec(expand.shape, lambda bi, h, qi: (0, 0, 0))],
        out_specs=pl.BlockSpec((1, tq, qw), lambda bi, h, qi: (bi, qi, h)),
        out_shape=jax.ShapeDtypeStruct((b, s, NSA_HEADS * HEAD_DIM), BF16),
        compiler_params=_cparams(("parallel", "parallel", "arbitrary"), 48),
        name="nsa_attention",
    )(qkv, qkv, qkv, qkv, qkv, kvcmp, kvcmp, hconv, gate_b, expand)


SWA_TQ = 256


def _swa_kernel(sink_ref, q_ref, kp_ref, kc_ref, vp_ref, vc_ref, o_ref):
    qi = pl.program_id(1)
    tq = q_ref.shape[1]
    grp = SWA_GROUP
    t0 = qi * tq
    tpos = t0 + lax.broadcasted_iota(jnp.int32, (tq, 1), 0)
    kpos = t0 - SWA_WINDOW + lax.broadcasted_iota(jnp.int32, (1, SWA_WINDOW + tq), 1)
    bias = jnp.where((kpos >= 0) & (kpos <= tpos) & (kpos > tpos - SWA_WINDOW), 0.0, NEG_INF)
    for h in range(SWA_KV_HEADS):
        kv = slice(h * HEAD_DIM, (h + 1) * HEAD_DIM)
        qs = _stack_heads(q_ref[0, :, h * grp * HEAD_DIM:(h + 1) * grp * HEAD_DIM], grp)
        k = jnp.concatenate([kp_ref[0, :, kv], kc_ref[0, :, kv]], axis=0)
        v = jnp.concatenate([vp_ref[0, :, kv], vc_ref[0, :, kv]], axis=0)
        s = lax.dot_general(qs, k, _NT, preferred_element_type=F32)
        s = jnp.concatenate([s[g * tq:(g + 1) * tq] + bias for g in range(grp)], axis=0)
        sink = jnp.concatenate([jnp.full((tq, 1), sink_ref[h * grp + g] * LOG2E, F32) for g in range(grp)], axis=0)
        m = jnp.maximum(jnp.max(s, axis=-1, keepdims=True), sink)
        p = jnp.exp2(s - m)
        den = jnp.sum(p, axis=-1, keepdims=True) + jnp.exp2(sink - m)
        o = jnp.dot(p.astype(BF16), v, preferred_element_type=F32) / den
        for g in range(grp):
            col = (h * grp + g) * HEAD_DIM
            o_ref[0, :, col:col + HEAD_DIM] = o[g * tq:(g + 1) * tq].astype(o_ref.dtype)


def _swa(qkv, sinks):
    b, s, _ = qkv.shape
    tq = _tile(s, SWA_TQ)
    per = tq // SWA_WINDOW
    qw = SWA_HEADS * HEAD_DIM
    kw = SWA_KV_HEADS * HEAD_DIM
    assert QKV_SWA_Q == 0 and QKV_SWA_K % SWA_KV_HEADS == 0 and QKV_SWA_V % SWA_KV_HEADS == 0
    prev = lambda chunk: pl.BlockSpec((1, SWA_WINDOW, kw),
                                      lambda bi, qi: (bi, jnp.maximum(qi * per - 1, 0), chunk // SWA_KV_HEADS))
    cur = lambda chunk: pl.BlockSpec((1, tq, kw), lambda bi, qi: (bi, qi, chunk // SWA_KV_HEADS))
    return pl.pallas_call(
        _swa_kernel,
        grid=(b, s // tq),
        in_specs=[pl.BlockSpec(memory_space=pltpu.SMEM),
                  pl.BlockSpec((1, tq, qw), lambda bi, qi: (bi, qi, 0)),
                  prev(QKV_SWA_K), cur(QKV_SWA_K), prev(QKV_SWA_V), cur(QKV_SWA_V)],
        out_specs=pl.BlockSpec((1, tq, qw), lambda bi, qi: (bi, qi, 0)),
        out_shape=jax.ShapeDtypeStruct((b, s, qw), BF16),
        compiler_params=_cparams(("parallel", "arbitrary"), 32),
        name="swa_attention",
    )(sinks, qkv, qkv, qkv, qkv, qkv)


def _split_in_proj(w_in, gate_b):
    sizes = (CONV_CH, CONV_CH, NSA_HEADS * HEAD_DIM) + (NSA_KV_HEADS * HEAD_DIM,) * 6 + \
            (NSA_HEADS * 3, SWA_HEADS * HEAD_DIM, SWA_KV_HEADS * HEAD_DIM, SWA_KV_HEADS * HEAD_DIM)
    off = np.concatenate([[0], np.cumsum(sizes)])
    seg = lambda i: w_in[:, off[i]:off[i + 1]]
    a_val, a_gate, q_b, kc, vc, ksl, vsl, kw, vw, g_b, q_c, k_c, v_c = (seg(i) for i in range(13))
    gap = jnp.zeros((w_in.shape[0], 2 * HEAD_DIM), F32)
    w_qkv = jnp.concatenate([q_c, q_b, ksl, kw, k_c, gap, vsl, vw, v_c, gap], axis=1).astype(BF16)
    assert w_qkv.shape[1] == QKV_CHUNKS * HEAD_DIM
    w_cmp = jnp.concatenate([kc, vc], axis=1).astype(BF16)
    per_head = NSA_GROUP * 3
    pad_w = jnp.zeros((w_in.shape[0], LANES - per_head), F32)
    pad_b = jnp.zeros((LANES - per_head,), F32)
    gate_w, gate_bias = [], []
    for h in range(NSA_KV_HEADS):
        gate_w += [g_b[:, h * per_head:(h + 1) * per_head], pad_w]
        gate_bias += [gate_b[h * per_head:(h + 1) * per_head], pad_b]
    w_conv = jnp.concatenate([a_val, a_gate] + gate_w, axis=1).astype(BF16)
    return w_qkv, w_cmp, w_conv, jnp.concatenate(gate_bias).reshape(1, NSA_KV_HEADS * LANES)


def _ffn_block(x2, xb, w_gu, w_down, layer, ln_g, ln_b):
    hidden = _ffn_up(xb, w_gu, layer, w_down.shape[1])
    return _mm_res_ln(hidden, w_down, layer, x2, ln_g, ln_b, 0.5)


def _mixer_block(x2, xb, rope, batch, w_in, w_out, layer, dw_w, dw_b, cln_g, cln_b, cmp_pos, cmp_w1, cmp_w2,
                 gate_b, sinks, ln_g, ln_b):
    m = x2.shape[0]
    s = m // batch
    w_qkv, w_cmp, w_conv, gate_bias = _split_in_proj(w_in, gate_b)
    qkv = _proj(xb, w_qkv, rope, QKV_QUERY_CHUNKS, QKV_ROPE_CHUNKS, BF16).reshape(batch, s, -1)
    hcmp = _proj(xb, w_cmp, rope, 0, NSA_KV_HEADS, F32)
    hconv = _proj(xb, w_conv, rope, 0, 0, F32).reshape(batch, s, -1)
    out_a = _conv_module(hconv, dw_w, dw_b, cln_g, cln_b)
    t4 = hcmp.reshape(batch, s // CMP_STRIDE, CMP_STRIDE, 2, NSA_KV_HEADS, HEAD_DIM)
    t4 = t4.transpose(0, 3, 4, 1, 2, 5).reshape(batch, 2, NSA_KV_HEADS, s // CMP_STRIDE, CMP_STRIDE * HEAD_DIM)
    pos = cmp_pos.reshape(2, 2, CMP_STRIDE * HEAD_DIM)
    kvcmp = _compress(t4, pos, cmp_w1.astype(BF16), cmp_w2.astype(BF16))
    out_b = _nsa(qkv, kvcmp, hconv, gate_bias)
    out_c = _swa(qkv, sinks)
    mixed = jnp.concatenate([out_a, out_b, out_c], axis=-1).reshape(m, -1)
    return _mm_res_ln(mixed, w_out, layer, x2, ln_g, ln_b, 1.0)


def _rope_tables(positions):
    inv = jnp.power(ROPE_THETA, -jnp.arange(0, HEAD_DIM, 2, dtype=F32) / HEAD_DIM)
    ang = positions.astype(F32).reshape(-1, 1) * inv
    cos, sin = jnp.cos(ang), jnp.sin(ang)
    table = jnp.concatenate([cos, cos, -sin, sin], axis=-1)
    return jnp.stack([table, table * QUERY_SCALE])


def kernel(x, positions, ffn1_w_gu, ffn1_w_down, ln1_g, ln1_b, w_in, conv_dw_w, conv_dw_b, conv_ln_g, conv_ln_b,
           nsa_cmp_pos, nsa_cmp_w1, nsa_cmp_w2, nsa_gate_b, swa_sinks, w_out, ln2_g, ln2_b,
           ffn2_w_gu, ffn2_w_down, ln3_g, ln3_b):
    batch, s, d = x.shape
    rope = _rope_tables(positions)
    x2 = x.reshape(batch * s, d)
    xb = x2.astype(BF16)
    gu1, down1, gu2, down2, out_w = (w.astype(BF16) for w in (ffn1_w_gu, ffn1_w_down, ffn2_w_gu, ffn2_w_down, w_out))
    for i in range(w_in.shape[0]):
        x2, xb = _ffn_block(x2, xb, gu1, down1, i, ln1_g[i], ln1_b[i])
        x2, xb = _mixer_block(x2, xb, rope, batch, w_in[i], out_w, i, conv_dw_w[i], conv_dw_b[i], conv_ln_g[i],
                              conv_ln_b[i], nsa_cmp_pos[i], nsa_cmp_w1[i], nsa_cmp_w2[i], nsa_gate_b[i],
                              swa_sinks[i], ln2_g[i], ln2_b[i])
        x2, xb = _ffn_block(x2, xb, gu2, down2, i, ln3_g[i], ln3_b[i])
    return x2.reshape(batch, s, d)
```

```python
import functools

import jax
import jax.numpy as jnp
import numpy as np
from jax import lax
from jax.experimental import pallas as pl
from jax.experimental.pallas import tpu as pltpu

F32 = jnp.float32
BF16 = jnp.bfloat16

HEAD_DIM = 128
CONV_CH = 1024
CONV_WIDTH = 31
NSA_HEADS = 12
NSA_KV_HEADS = 4
NSA_GROUP = NSA_HEADS // NSA_KV_HEADS
SWA_HEADS = 12
SWA_KV_HEADS = 2
SWA_GROUP = SWA_HEADS // SWA_KV_HEADS
CMP_BLOCK = 32
CMP_STRIDE = 16
SEL_BLOCK = 64
SEL_SHIFT = 6
SEL_TOPK = 16
NSA_WINDOW = 512
SWA_WINDOW = 128
ROPE_THETA = 10000.0
LN_EPS = 1e-5
DEPTH = 2
DEEPNORM_ALPHA = (2 * DEPTH) ** 0.25
NEG_INF = -1e30
FORCE_SCORE = 1e4
PICKED = -3e38
ATTN_SCALE = HEAD_DIM ** -0.5
LOG2E = 1.4426950408889634
QUERY_SCALE = ATTN_SCALE * LOG2E

V7X_VMEM_BYTES = 64 * 1024 * 1024
LANES = 128
SUBLANES = 8

QKV_SWA_Q = 0
QKV_NSA_Q = 12
QKV_QUERY_CHUNKS = 24
QKV_KSL = 24
QKV_KWIN = 28
QKV_SWA_K = 32
QKV_ROPE_CHUNKS = 36
QKV_VSL = 36
QKV_VWIN = 40
QKV_SWA_V = 44
QKV_CHUNKS = 48
PROJ_TN = 512
CONV_GATE_CHUNK = 2 * CONV_CH // LANES


def _cparams(semantics, vmem_mib):
    return pltpu.CompilerParams(dimension_semantics=semantics, vmem_limit_bytes=vmem_mib * 1024 * 1024)


def _tile(n, pref):
    t = min(n, pref)
    while n % t:
        t //= 2
    return t


FFN_WBLK = 256
FFN_HIDDEN_ALIGN = 1024


def _gu_kernel(x_ref, wg0_ref, wg1_ref, wu0_ref, wu1_ref, o_ref, *, d_ff):
    j = pl.program_id(1)
    wb = wg0_ref.shape[2]
    n_real = d_ff // wb
    halves = ((wg0_ref, wu0_ref), (wg1_ref, wu1_ref))

    def step(n_live):
        x = x_ref[...]
        for half, (wg_ref, wu_ref) in enumerate(halves):
            cols = slice(half * wb, (half + 1) * wb)
            if half < n_live:
                g = jnp.dot(x, wg_ref[0], preferred_element_type=F32)
                u = jnp.dot(x, wu_ref[0], preferred_element_type=F32)
                o_ref[:, cols] = (g * jax.nn.sigmoid(g) * u).astype(o_ref.dtype)
            else:
                o_ref[:, cols] = jnp.zeros((o_ref.shape[0], wb), o_ref.dtype)

    pl.when(j < n_real // 2)(lambda: step(2))
    pl.when(j == n_real // 2)(lambda: step(n_real % 2))
    pl.when(j > n_real // 2)(lambda: step(0))


def _ffn_up(xb, w_gu, layer, d_ff):
    m, d = xb.shape
    wb = FFN_WBLK
    assert d_ff % wb == 0
    fp = -(-d_ff // FFN_HIDDEN_ALIGN) * FFN_HIDDEN_ALIGN
    up0 = d_ff // wb
    last = 2 * d_ff // wb - 1
    tm = _tile(m, 1024)
    wspec = lambda f: pl.BlockSpec((1, d, wb), lambda i, j: (layer, 0, jnp.minimum(f(j), last)))
    return pl.pallas_call(
        functools.partial(_gu_kernel, d_ff=d_ff),
        grid=(m // tm, fp // (2 * wb)),
        in_specs=[pl.BlockSpec((tm, d), lambda i, j: (i, 0)),
                  wspec(lambda j: 2 * j), wspec(lambda j: 2 * j + 1),
                  wspec(lambda j: up0 + 2 * j), wspec(lambda j: up0 + 2 * j + 1)],
        out_specs=pl.BlockSpec((tm, 2 * wb), lambda i, j: (i, j)),
        out_shape=jax.ShapeDtypeStruct((m, fp), BF16),
        compiler_params=_cparams(("parallel", "arbitrary"), 48),
        name="ffn_up",
    )(xb, w_gu, w_gu, w_gu, w_gu)


LN_ROWS = 32
LN_CHUNK = 128
MM_COLS = 1024


def _mm_res_ln_kernel(a_ref, w_ref, res_hbm, g_ref, b_ref, of_hbm, ob_hbm,
                      acc_ref, res_buf, of_buf, ob_buf, res_sem, of_sem, ob_sem, *, scale, nk, k_tail):
    i = pl.program_id(0)
    k = pl.program_id(1)
    tm, tk = a_ref.shape
    n_chunks = tm // LN_CHUNK
    row0 = i * tm

    def rows_of(c):
        return pl.ds(pl.multiple_of(row0 + c * LN_CHUNK, LN_CHUNK), LN_CHUNK)

    def res_copy(c):
        return pltpu.make_async_copy(res_hbm.at[rows_of(c), :], res_buf.at[c % 2], res_sem.at[c % 2])

    def of_copy(c):
        return pltpu.make_async_copy(of_buf.at[c % 2], of_hbm.at[rows_of(c), :], of_sem.at[c % 2])

    def ob_copy(c):
        return pltpu.make_async_copy(ob_buf.at[c % 2], ob_hbm.at[rows_of(c), :], ob_sem.at[c % 2])

    @pl.when(k == 0)
    def _():
        acc_ref[...] = jnp.zeros_like(acc_ref)

    @pl.when(k == nk - 1)
    def _():
        for c in range(min(2, n_chunks)):
            res_copy(c).start()

    def accumulate(valid_rows):
        a = a_ref[:, 0:valid_rows]
        for c in range(acc_ref.shape[1] // MM_COLS):
            cols = slice(c * MM_COLS, (c + 1) * MM_COLS)
            acc_ref[:, cols] += jnp.dot(a, w_ref[0, 0:valid_rows, cols], preferred_element_type=F32)

    if k_tail == tk:
        accumulate(tk)
    else:
        pl.when(k < nk - 1)(lambda: accumulate(tk))
        pl.when(k == nk - 1)(lambda: accumulate(k_tail))

    @pl.when(k == nk - 1)
    def _():
        gamma = g_ref[...]
        beta = b_ref[...]
        for c in range(n_chunks):
            slot = c % 2
            res_copy(c).wait()
            if c >= 2:
                of_copy(c - 2).wait()
                ob_copy(c - 2).wait()

            def rows(r, carry):
                sl = pl.ds(pl.multiple_of(r * LN_ROWS, LN_ROWS), LN_ROWS)
                src = pl.ds(pl.multiple_of(c * LN_CHUNK + r * LN_ROWS, LN_ROWS), LN_ROWS)
                y = DEEPNORM_ALPHA * res_buf[slot, sl, :] + scale * acc_ref[src, :]
                mu = jnp.mean(y, axis=-1, keepdims=True)
                dlt = y - mu
                var = jnp.mean(dlt * dlt, axis=-1, keepdims=True)
                out = dlt * lax.rsqrt(var + LN_EPS) * gamma + beta
                of_buf[slot, sl, :] = out
                ob_buf[slot, sl, :] = out.astype(ob_buf.dtype)
                return carry

            lax.fori_loop(0, LN_CHUNK // LN_ROWS, rows, 0)
            of_copy(c).start()
            ob_copy(c).start()
            if c + 2 < n_chunks:
                res_copy(c + 2).start()
        for c in range(max(n_chunks - 2, 0), n_chunks):
            of_copy(c).wait()
            ob_copy(c).wait()


def _mm_res_ln(a, w, layer, res, gamma, beta, scale):
    m, kp = a.shape
    _, k_valid, d = w.shape
    tm, tk = _tile(m, 1024), _tile(kp, 1024)
    nk = kp // tk
    assert (nk - 1) * tk < k_valid <= kp and k_valid % LANES == 0 and tm % LN_CHUNK == 0 and d % MM_COLS == 0
    hbm = pl.BlockSpec(memory_space=pl.ANY)
    return pl.pallas_call(
        functools.partial(_mm_res_ln_kernel, scale=scale, nk=nk, k_tail=k_valid - (nk - 1) * tk),
        grid=(m // tm, nk),
        in_specs=[pl.BlockSpec((tm, tk), lambda i, k: (i, k)),
                  pl.BlockSpec((1, tk, d), lambda i, k: (layer, k, 0)),
                  hbm,
                  pl.BlockSpec((1, d), lambda i, k: (0, 0)),
                  pl.BlockSpec((1, d), lambda i, k: (0, 0))],
        out_specs=[hbm, hbm],
        out_shape=[jax.ShapeDtypeStruct((m, d), F32), jax.ShapeDtypeStruct((m, d), BF16)],
        scratch_shapes=[pltpu.VMEM((tm, d), F32),
                        pltpu.VMEM((2, LN_CHUNK, d), F32), pltpu.VMEM((2, LN_CHUNK, d), F32),
                        pltpu.VMEM((2, LN_CHUNK, d), BF16),
                        pltpu.SemaphoreType.DMA((2,)), pltpu.SemaphoreType.DMA((2,)), pltpu.SemaphoreType.DMA((2,))],
        compiler_params=_cparams(("arbitrary", "arbitrary"), 56),
        name="mm_res_ln",
    )(a, w, res, gamma.reshape(1, d), beta.reshape(1, d))


def _proj_kernel(x_ref, w_ref, rope_ref, o_ref, *, n_query_tiles, n_rope_tiles):
    j = pl.program_id(1)
    acc = jnp.dot(x_ref[...], w_ref[...], preferred_element_type=F32)
    tn = acc.shape[1]

    def rope(table):
        cc = rope_ref[table, :, 0:HEAD_DIM]
        ss = rope_ref[table, :, HEAD_DIM:2 * HEAD_DIM]
        for c in range(tn // HEAD_DIM):
            head = acc[:, c * HEAD_DIM:(c + 1) * HEAD_DIM]
            rot = head * cc + pltpu.roll(head, HEAD_DIM // 2, 1) * ss
            o_ref[:, c * HEAD_DIM:(c + 1) * HEAD_DIM] = rot.astype(o_ref.dtype)

    pl.when(j < n_query_tiles)(lambda: rope(1))
    pl.when((j >= n_query_tiles) & (j < n_rope_tiles))(lambda: rope(0))

    @pl.when(j >= n_rope_tiles)
    def _():
        o_ref[...] = acc.astype(o_ref.dtype)


def _proj(xb, w, rope_tables, n_query_chunks, n_rope_chunks, out_dtype):
    m, d = xb.shape
    n = w.shape[1]
    tm, tn = _tile(m, 1024), PROJ_TN
    assert n % tn == 0 and (n_rope_chunks * HEAD_DIM) % tn == 0 and (n_query_chunks * HEAD_DIM) % tn == 0
    return pl.pallas_call(
        functools.partial(_proj_kernel, n_query_tiles=n_query_chunks * HEAD_DIM // tn,
                          n_rope_tiles=n_rope_chunks * HEAD_DIM // tn),
        grid=(m // tm, n // tn),
        in_specs=[pl.BlockSpec((tm, d), lambda i, j: (i, 0)),
                  pl.BlockSpec((d, tn), lambda i, j: (0, j)),
                  pl.BlockSpec((2, tm, 2 * HEAD_DIM), lambda i, j: (0, i, 0))],
        out_specs=pl.BlockSpec((tm, tn), lambda i, j: (i, j)),
        out_shape=jax.ShapeDtypeStruct((m, n), out_dtype),
        compiler_params=_cparams(("parallel", "arbitrary"), 40),
        name="in_proj",
    )(xb, w, rope_tables)


CONV_HALO = 32
CONV_TS = 128


def _conv_kernel(v_ref, g_ref, vh_ref, gh_ref, w_ref, b_ref, lg_ref, lb_ref, o_ref, ext_ref, shift_ref):
    i = pl.program_id(1)
    ts = v_ref.shape[1]
    halo = vh_ref[0] * jax.nn.sigmoid(gh_ref[0])
    ext_ref[0:CONV_HALO, :] = jnp.where(i > 0, halo, 0.0)
    ext_ref[CONV_HALO:, :] = v_ref[0] * jax.nn.sigmoid(g_ref[0])
    rows = shift_ref.shape[1]
    for r in range(1, SUBLANES):
        shift_ref[r - 1] = ext_ref[r:r + rows, :]
    first = CONV_HALO - (CONV_WIDTH - 1)
    cols = []
    for c in range(CONV_CH // LANES):
        lanes = slice(c * LANES, (c + 1) * LANES)
        acc = jnp.zeros((ts, LANES), F32) + b_ref[:, lanes]
        for k in range(CONV_WIDTH):
            base, r = (first + k) // SUBLANES * SUBLANES, (first + k) % SUBLANES
            window = ext_ref[base:base + ts, lanes] if r == 0 else shift_ref[r - 1, base:base + ts, lanes]
            acc = acc + w_ref[k:k + 1, lanes] * window
        cols.append(acc)
    y = jnp.concatenate(cols, axis=1)
    mu = jnp.mean(y, axis=-1, keepdims=True)
    dlt = y - mu
    var = jnp.mean(dlt * dlt, axis=-1, keepdims=True)
    z = dlt * lax.rsqrt(var + LN_EPS) * lg_ref[...] + lb_ref[...]
    o_ref[0] = (z * jax.nn.sigmoid(z)).astype(o_ref.dtype)


def _conv_module(hconv, dw_w, dw_b, ln_g, ln_b):
    b, s, _ = hconv.shape
    ts = _tile(s, CONV_TS)
    per = ts // CONV_HALO
    w_pad = jnp.concatenate([dw_w, jnp.zeros((CONV_HALO - CONV_WIDTH, CONV_CH), F32)], axis=0)
    vec = lambda a: a.reshape(1, CONV_CH)
    const = lambda shape: pl.BlockSpec(shape, lambda bi, i: (0, 0))
    return pl.pallas_call(
        _conv_kernel,
        grid=(b, s // ts),
        in_specs=[pl.BlockSpec((1, ts, CONV_CH), lambda bi, i: (bi, i, 0)),
                  pl.BlockSpec((1, ts, CONV_CH), lambda bi, i: (bi, i, 1)),
                  pl.BlockSpec((1, CONV_HALO, CONV_CH), lambda bi, i: (bi, jnp.maximum(i * per - 1, 0), 0)),
                  pl.BlockSpec((1, CONV_HALO, CONV_CH), lambda bi, i: (bi, jnp.maximum(i * per - 1, 0), 1)),
                  const((CONV_HALO, CONV_CH)), const((1, CONV_CH)), const((1, CONV_CH)), const((1, CONV_CH))],
        out_specs=pl.BlockSpec((1, ts, CONV_CH), lambda bi, i: (bi, i, 0)),
        out_shape=jax.ShapeDtypeStruct((b, s, CONV_CH), BF16),
        scratch_shapes=[pltpu.VMEM((ts + CONV_HALO, CONV_CH), F32),
                        pltpu.VMEM((SUBLANES - 1, ts + CONV_HALO - SUBLANES, CONV_CH), F32)],
        compiler_params=_cparams(("parallel", "arbitrary"), 32),
        name="conv_module",
    )(hconv, hconv, hconv, hconv, w_pad, vec(dw_b), vec(ln_g), vec(ln_b))


def _compress_kernel(t_ref, pos_ref, w1_ref, w2_ref, o_ref):
    n = t_ref.shape[1] // CMP_STRIDE
    lo = jnp.zeros((n, HEAD_DIM), F32)
    hi = jnp.zeros((n, HEAD_DIM), F32)
    for l in range(CMP_STRIDE):
        tok = t_ref[0, pl.ds(l, n, stride=CMP_STRIDE), :]
        for part, first in ((0, l), (1, CMP_STRIDE + l)):
            x = (tok + pos_ref[0, first:first + 1, :]).astype(BF16)
            y = jnp.dot(x, w1_ref[0, first * HEAD_DIM:(first + 1) * HEAD_DIM, :], preferred_element_type=F32)
            lo, hi = (lo + y, hi) if part == 0 else (lo, hi + y)
    h1 = lo + pltpu.roll(hi, n - 1, 0)
    h1 = h1 * jax.nn.sigmoid(h1)
    out = jnp.dot(h1.astype(BF16), w2_ref[0], preferred_element_type=F32)
    row = lax.broadcasted_iota(jnp.int32, out.shape, 0)
    o_ref[0, 0, 0] = jnp.where(row < n - 1, out, 0.0).astype(o_ref.dtype)


def _compress(hcmp, pos, w1, w2):
    b, s, _ = hcmp.shape
    hkv = NSA_KV_HEADS
    n = s // CMP_STRIDE
    return pl.pallas_call(
        _compress_kernel,
        grid=(b, 2, hkv),
        in_specs=[pl.BlockSpec((1, s, HEAD_DIM), lambda bi, kv, h: (bi, 0, kv * hkv + h)),
                  pl.BlockSpec((1, CMP_BLOCK, HEAD_DIM), lambda bi, kv, h: (kv, 0, 0)),
                  pl.BlockSpec((1, CMP_BLOCK * HEAD_DIM, HEAD_DIM), lambda bi, kv, h: (kv, 0, 0)),
                  pl.BlockSpec((1, HEAD_DIM, HEAD_DIM), lambda bi, kv, h: (kv, 0, 0))],
        out_specs=pl.BlockSpec((1, 1, 1, n, HEAD_DIM), lambda bi, kv, h: (bi, kv, h, 0, 0)),
        out_shape=jax.ShapeDtypeStruct((b, 2, hkv, n, HEAD_DIM), BF16),
        compiler_params=_cparams(("parallel", "parallel", "arbitrary"), 40),
        name="nsa_compress",
    )(hcmp, pos, w1, w2)


_NT = (((1,), (1,)), ((), ()))


def _stack_heads(q, group):
    return jnp.concatenate([q[:, g * HEAD_DIM:(g + 1) * HEAD_DIM] for g in range(group)], axis=0)


def _online_step(qs, k, v, bias, carry):
    m, l, acc = carry
    s = lax.dot_general(qs, k, _NT, preferred_element_type=F32)
    if bias is not None:
        tq = bias.shape[0]
        s = jnp.concatenate([s[r:r + tq] + bias for r in range(0, s.shape[0], tq)], axis=0)
    m_new = jnp.maximum(m, jnp.max(s, axis=-1, keepdims=True))
    alpha = jnp.exp2(m - m_new)
    p = jnp.exp2(s - m_new)
    l_new = alpha * l + jnp.sum(p, axis=-1, keepdims=True)
    acc_new = alpha * acc + jnp.dot(p.astype(BF16), v, preferred_element_type=F32)
    return m_new, l_new, acc_new


def _online_init(rows):
    return (jnp.full((rows, 1), NEG_INF, F32), jnp.zeros((rows, 1), F32), jnp.zeros((rows, HEAD_DIM), F32))


NSA_TQ = 256
NSA_TK = 512
SEL_UNROLL = 4


def _nsa_kernel(q_ref, ksl_ref, vsl_ref, kw_ref, vw_ref, kc_ref, vc_ref, gl_ref, gb_ref, ex_ref, o_ref, *, tk_sel):
    qi = pl.program_id(2)
    tq = q_ref.shape[1]
    seq = ksl_ref.shape[1]
    n_cmp = kc_ref.shape[3]
    n_sel = seq // SEL_BLOCK
    grp = NSA_GROUP
    t0 = qi * tq
    qs = _stack_heads(q_ref[0], grp)
    tpos = t0 + lax.broadcasted_iota(jnp.int32, (tq, 1), 0)

    cmp_end = lax.broadcasted_iota(jnp.int32, (1, n_cmp), 1) * CMP_STRIDE + (CMP_BLOCK - 1)
    valid_c = cmp_end <= tpos
    s_c = lax.dot_general(qs, kc_ref[0, 0, 0], _NT, preferred_element_type=F32)
    p_parts, p_sum = [], None
    for g in range(grp):
        sm = jnp.where(valid_c, s_c[g * tq:(g + 1) * tq], NEG_INF)
        p = jnp.where(valid_c, jnp.exp2(sm - jnp.max(sm, axis=-1, keepdims=True)), 0.0)
        den = jnp.sum(p, axis=-1, keepdims=True)
        p_c = p / jnp.where(den > 0.0, den, 1.0)
        p_parts.append(p_c.astype(BF16))
        p_sum = p_c if p_sum is None else p_sum + p_c
    o_c = jnp.dot(jnp.concatenate(p_parts, axis=0), vc_ref[0, 0, 0], preferred_element_type=F32)

    n_i = lax.broadcasted_iota(jnp.int32, (n_cmp, n_sel), 0)
    j_i = lax.broadcasted_iota(jnp.int32, (n_cmp, n_sel), 1)
    ratio = SEL_BLOCK // CMP_STRIDE
    overlap = jnp.where((n_i >= ratio * j_i - (CMP_BLOCK // CMP_STRIDE - 1)) & (n_i <= ratio * j_i + ratio - 1),
                        1.0, 0.0).astype(BF16)
    p_hi = p_sum.astype(BF16)
    p_lo = (p_sum - p_hi.astype(F32)).astype(BF16)
    imp = (jnp.dot(p_hi, overlap, preferred_element_type=F32) + jnp.dot(p_lo, overlap, preferred_element_type=F32))
    blk = lax.broadcasted_iota(jnp.int32, (1, n_sel), 1)
    t_blk = jnp.right_shift(tpos, SEL_SHIFT)
    forced = (blk == 0) | (blk == t_blk) | (blk == t_blk - 1)
    imp = jnp.where(forced, FORCE_SCORE, imp)
    imp = jnp.where(blk <= t_blk, imp, NEG_INF)

    imp_t = imp.T
    rowf = lax.broadcasted_iota(jnp.int32, (n_sel, tq), 0).astype(F32)
    sel_t = jnp.zeros((n_sel, tq), F32)
    for _ in range(min(SEL_TOPK, n_sel)):
        mx = jnp.max(imp_t, axis=0, keepdims=True)
        idx = jnp.min(jnp.where(imp_t == mx, rowf, float(n_sel)), axis=0, keepdims=True)
        pick = rowf == idx
        sel_t = jnp.where(pick, 1.0, sel_t)
        imp_t = jnp.where(pick, PICKED, imp_t)
    not_sel = (1.0 - sel_t.T).astype(BF16)

    qa = jnp.concatenate([qs, jnp.concatenate([not_sel] * grp, axis=0)], axis=1)

    def sel_step(kt, carry, causal):
        k0 = pl.multiple_of(kt * tk_sel, tk_sel)
        ka = jnp.concatenate([ksl_ref[0, pl.ds(k0, tk_sel), :], ex_ref[kt]], axis=1)
        bias = None
        if causal:
            kpos = k0 + lax.broadcasted_iota(jnp.int32, (1, tk_sel), 1)
            bias = jnp.where(kpos <= tpos, 0.0, NEG_INF)
        return _online_step(qa, ka, vsl_ref[0, pl.ds(k0, tk_sel), :], bias, carry)

    n_full = t0 // tk_sel
    full_step = functools.partial(sel_step, causal=False)
    def group_step(i, carry):
        for u in range(SEL_UNROLL):
            carry = full_step(SEL_UNROLL * i + u, carry)
        return carry

    n_groups = n_full // SEL_UNROLL
    sel_out = lax.fori_loop(0, n_groups, group_step, _online_init(grp * tq))
    sel_out = lax.fori_loop(SEL_UNROLL * n_groups, n_full, full_step, sel_out)
    sel_out = sel_step(n_full, sel_out, causal=True)
    o_s = sel_out[2] / sel_out[1]

    span = NSA_WINDOW + tq
    k0 = pl.multiple_of(jnp.maximum(t0 - NSA_WINDOW, 0), tq)
    kpos = k0 + lax.broadcasted_iota(jnp.int32, (1, span), 1)
    bias = jnp.where((kpos <= tpos) & (kpos > tpos - NSA_WINDOW), 0.0, NEG_INF)
    win_out = _online_step(qs, kw_ref[0, pl.ds(k0, span), :], vw_ref[0, pl.ds(k0, span), :], bias,
                           _online_init(grp * tq))

    gate = jax.nn.sigmoid(gl_ref[0] + gb_ref[...])
    o_w = win_out[2] / win_out[1]
    for g in range(grp):
        rows = slice(g * tq, (g + 1) * tq)
        o = (gate[:, 3 * g:3 * g + 1] * o_c[rows] + gate[:, 3 * g + 1:3 * g + 2] * o_s[rows]
             + gate[:, 3 * g + 2:3 * g + 3] * o_w[rows])
        o_ref[0, :, g * HEAD_DIM:(g + 1) * HEAD_DIM] = o.astype(o_ref.dtype)


def _nsa(qkv, kvcmp, hconv, gate_b):
    b, s, _ = qkv.shape
    n = kvcmp.shape[3]
    tq = _tile(s, NSA_TQ)
    tk_sel = _tile(s, NSA_TK)
    assert NSA_WINDOW % tq == 0 and NSA_WINDOW + tq <= s and tk_sel % SEL_BLOCK == 0
    qw = NSA_GROUP * HEAD_DIM
    n_sel = s // SEL_BLOCK
    assert tk_sel % tq == 0
    key_block = np.arange(s).reshape(s // tk_sel, tk_sel, 1) // SEL_BLOCK
    onehot = key_block == np.arange(n_sel).reshape(1, 1, n_sel)
    expand = jnp.asarray(np.where(onehot, NEG_INF, 0.0), BF16)
    full = lambda chunk: pl.BlockSpec((1, s, HEAD_DIM), lambda bi, h, qi: (bi, 0, chunk + h))
    return pl.pallas_call(
        functools.partial(_nsa_kernel, tk_sel=tk_sel),
        grid=(b, NSA_KV_HEADS, s // tq),
        in_specs=[pl.BlockSpec((1, tq, qw), lambda bi, h, qi: (bi, qi, QKV_NSA_Q // NSA_GROUP + h)),
                  full(QKV_KSL), full(QKV_VSL), full(QKV_KWIN), full(QKV_VWIN),
                  pl.BlockSpec((1, 1, 1, n, HEAD_DIM), lambda bi, h, qi: (bi, 0, h, 0, 0)),
                  pl.BlockSpec((1, 1, 1, n, HEAD_DIM), lambda bi, h, qi: (bi, 1, h, 0, 0)),
                  pl.BlockSpec((1, tq, LANES), lambda bi, h, qi: (bi, qi, CONV_GATE_CHUNK + h)),
                  pl.BlockSpec((1, LANES), lambda bi, h, qi: (0, h)),
                  pl.BlockSpec(expand.shape, lambda bi, h, qi: (0, 0, 0))],
        out_specs=pl.BlockSpec((1, tq, qw), lambda bi, h, qi: (bi, qi, h)),
        out_shape=jax.ShapeDtypeStruct((b, s, NSA_HEADS * HEAD_DIM), BF16),
        compiler_params=_cparams(("parallel", "parallel", "arbitrary"), 48),
        name="nsa_attention",
    )(qkv, qkv, qkv, qkv, qkv, kvcmp, kvcmp, hconv, gate_b, expand)


SWA_TQ = 256


def _swa_kernel(sink_ref, q_ref, kp_ref, kc_ref, vp_ref, vc_ref, o_ref):
    qi = pl.program_id(1)
    tq = q_ref.shape[1]
    grp = SWA_GROUP
    t0 = qi * tq
    tpos = t0 + lax.broadcasted_iota(jnp.int32, (tq, 1), 0)
    kpos = t0 - SWA_WINDOW + lax.broadcasted_iota(jnp.int32, (1, SWA_WINDOW + tq), 1)
    bias = jnp.where((kpos >= 0) & (kpos <= tpos) & (kpos > tpos - SWA_WINDOW), 0.0, NEG_INF)
    for h in range(SWA_KV_HEADS):
        kv = slice(h * HEAD_DIM, (h + 1) * HEAD_DIM)
        qs = _stack_heads(q_ref[0, :, h * grp * HEAD_DIM:(h + 1) * grp * HEAD_DIM], grp)
        k = jnp.concatenate([kp_ref[0, :, kv], kc_ref[0, :, kv]], axis=0)
        v = jnp.concatenate([vp_ref[0, :, kv], vc_ref[0, :, kv]], axis=0)
        s = lax.dot_general(qs, k, _NT, preferred_element_type=F32)
        s = jnp.concatenate([s[g * tq:(g + 1) * tq] + bias for g in range(grp)], axis=0)
        sink = jnp.concatenate([jnp.full((tq, 1), sink_ref[h * grp + g] * LOG2E, F32) for g in range(grp)], axis=0)
        m = jnp.maximum(jnp.max(s, axis=-1, keepdims=True), sink)
        p = jnp.exp2(s - m)
        den = jnp.sum(p, axis=-1, keepdims=True) + jnp.exp2(sink - m)
        o = jnp.dot(p.astype(BF16), v, preferred_element_type=F32) / den
        for g in range(grp):
            col = (h * grp + g) * HEAD_DIM
            o_ref[0, :, col:col + HEAD_DIM] = o[g * tq:(g + 1) * tq].astype(o_ref.dtype)


def _swa(qkv, sinks):
    b, s, _ = qkv.shape
    tq = _tile(s, SWA_TQ)
    per = tq // SWA_WINDOW
    qw = SWA_HEADS * HEAD_DIM
    kw = SWA_KV_HEADS * HEAD_DIM
    assert QKV_SWA_Q == 0 and QKV_SWA_K % SWA_KV_HEADS == 0 and QKV_SWA_V % SWA_KV_HEADS == 0
    prev = lambda chunk: pl.BlockSpec((1, SWA_WINDOW, kw),
                                      lambda bi, qi: (bi, jnp.maximum(qi * per - 1, 0), chunk // SWA_KV_HEADS))
    cur = lambda chunk: pl.BlockSpec((1, tq, kw), lambda bi, qi: (bi, qi, chunk // SWA_KV_HEADS))
    return pl.pallas_call(
        _swa_kernel,
        grid=(b, s // tq),
        in_specs=[pl.BlockSpec(memory_space=pltpu.SMEM),
                  pl.BlockSpec((1, tq, qw), lambda bi, qi: (bi, qi, 0)),
                  prev(QKV_SWA_K), cur(QKV_SWA_K), prev(QKV_SWA_V), cur(QKV_SWA_V)],
        out_specs=pl.BlockSpec((1, tq, qw), lambda bi, qi: (bi, qi, 0)),
        out_shape=jax.ShapeDtypeStruct((b, s, qw), BF16),
        compiler_params=_cparams(("parallel", "arbitrary"), 32),
        name="swa_attention",
    )(sinks, qkv, qkv, qkv, qkv, qkv)


def _split_in_proj(w_in, gate_b):
    sizes = (CONV_CH, CONV_CH, NSA_HEADS * HEAD_DIM) + (NSA_KV_HEADS * HEAD_DIM,) * 6 + \
            (NSA_HEADS * 3, SWA_HEADS * HEAD_DIM, SWA_KV_HEADS * HEAD_DIM, SWA_KV_HEADS * HEAD_DIM)
    off = np.concatenate([[0], np.cumsum(sizes)])
    seg = lambda i: w_in[:, off[i]:off[i + 1]]
    a_val, a_gate, q_b, kc, vc, ksl, vsl, kw, vw, g_b, q_c, k_c, v_c = (seg(i) for i in range(13))
    gap = jnp.zeros((w_in.shape[0], 2 * HEAD_DIM), F32)
    w_qkv = jnp.concatenate([q_c, q_b, ksl, kw, k_c, gap, vsl, vw, v_c, gap], axis=1).astype(BF16)
    assert w_qkv.shape[1] == QKV_CHUNKS * HEAD_DIM
    w_cmp = jnp.concatenate([kc, vc], axis=1).astype(BF16)
    per_head = NSA_GROUP * 3
    pad_w = jnp.zeros((w_in.shape[0], LANES - per_head), F32)
    pad_b = jnp.zeros((LANES - per_head,), F32)
    gate_w, gate_bias = [], []
    for h in range(NSA_KV_HEADS):
        gate_w += [g_b[:, h * per_head:(h + 1) * per_head], pad_w]
        gate_bias += [gate_b[h * per_head:(h + 1) * per_head], pad_b]
    w_conv = jnp.concatenate([a_val, a_gate] + gate_w, axis=1).astype(BF16)
    return w_qkv, w_cmp, w_conv, jnp.concatenate(gate_bias).reshape(1, NSA_KV_HEADS * LANES)


def _ffn_block(x2, xb, w_gu, w_down, layer, ln_g, ln_b):
    hidden = _ffn_up(xb, w_gu, layer, w_down.shape[1])
    return _mm_res_ln(hidden, w_down, layer, x2, ln_g, ln_b, 0.5)


def _mixer_block(x2, xb, rope, batch, w_in, w_out, layer, dw_w, dw_b, cln_g, cln_b, cmp_pos, cmp_w1, cmp_w2,
                 gate_b, sinks, ln_g, ln_b):
    m = x2.shape[0]
    s = m // batch
    w_qkv, w_cmp, w_conv, gate_bias = _split_in_proj(w_in, gate_b)
    qkv = _proj(xb, w_qkv, rope, QKV_QUERY_CHUNKS, QKV_ROPE_CHUNKS, BF16).reshape(batch, s, -1)
    hcmp = _proj(xb, w_cmp, rope, 0, NSA_KV_HEADS, F32)
    hconv = _proj(xb, w_conv, rope, 0, 0, F32).reshape(batch, s, -1)
    out_a = _conv_module(hconv, dw_w, dw_b, cln_g, cln_b)
    kvcmp = _compress(hcmp.reshape(batch, s, -1), cmp_pos, cmp_w1.astype(BF16), cmp_w2.astype(BF16))
    out_b = _nsa(qkv, kvcmp, hconv, gate_bias)
    out_c = _swa(qkv, sinks)
    mixed = jnp.concatenate([out_a, out_b, out_c], axis=-1).reshape(m, -1)
    return _mm_res_ln(mixed, w_out, layer, x2, ln_g, ln_b, 1.0)


def _rope_tables(positions):
    inv = jnp.power(ROPE_THETA, -jnp.arange(0, HEAD_DIM, 2, dtype=F32) / HEAD_DIM)
    ang = positions.astype(F32).reshape(-1, 1) * inv
    cos, sin = jnp.cos(ang), jnp.sin(ang)
    table = jnp.concatenate([cos, cos, -sin, sin], axis=-1)
    return jnp.stack([table, table * QUERY_SCALE])


def kernel(x, positions, ffn1_w_gu, ffn1_w_down, ln1_g, ln1_b, w_in, conv_dw_w, conv_dw_b, conv_ln_g, conv_ln_b,
           nsa_cmp_pos, nsa_cmp_w1, nsa_cmp_w2, nsa_gate_b, swa_sinks, w_out, ln2_g, ln2_b,
           ffn2_w_gu, ffn2_w_down, ln3_g, ln3_b):
    batch, s, d = x.shape
    rope = _rope_tables(positions)
    x2 = x.reshape(batch * s, d)
    xb = x2.astype(BF16)
    gu1, down1, gu2, down2, out_w = (w.astype(BF16) for w in (ffn1_w_gu, ffn1_w_down, ffn2_w_gu, ffn2_w_down, w_out))
    for i in range(w_in.shape[0]):
        x2, xb = _ffn_block(x2, xb, gu1, down1, i, ln1_g[i], ln1_b[i])
        x2, xb = _mixer_block(x2, xb, rope, batch, w_in[i], out_w, i, conv_dw_w[i], conv_dw_b[i], conv_ln_g[i],
                              conv_ln_b[i], nsa_cmp_pos[i], nsa_cmp_w1[i], nsa_cmp_w2[i], nsa_gate_b[i],
                              swa_sinks[i], ln2_g[i], ln2_b[i])
        x2, xb = _ffn_block(x2, xb, gu2, down2, i, ln3_g[i], ln3_b[i])
    return x2.reshape(batch, s, d)
```

```python
import functools

import jax
import jax.numpy as jnp
import numpy as np
from jax import lax
from jax.experimental import pallas as pl
from jax.experimental.pallas import tpu as pltpu

F32 = jnp.float32
BF16 = jnp.bfloat16

HEAD_DIM = 128
CONV_CH = 1024
CONV_WIDTH = 31
NSA_HEADS = 12
NSA_KV_HEADS = 4
NSA_GROUP = NSA_HEADS // NSA_KV_HEADS
SWA_HEADS = 12
SWA_KV_HEADS = 2
SWA_GROUP = SWA_HEADS // SWA_KV_HEADS
CMP_BLOCK = 32
CMP_STRIDE = 16
SEL_BLOCK = 64
SEL_SHIFT = 6
SEL_TOPK = 16
NSA_WINDOW = 512
SWA_WINDOW = 128
ROPE_THETA = 10000.0
LN_EPS = 1e-5
DEPTH = 2
DEEPNORM_ALPHA = (2 * DEPTH) ** 0.25
NEG_INF = -1e30
FORCE_SCORE = 1e4
PICKED = -3e38
ATTN_SCALE = HEAD_DIM ** -0.5
LOG2E = 1.4426950408889634
QUERY_SCALE = ATTN_SCALE * LOG2E

V7X_VMEM_BYTES = 64 * 1024 * 1024
LANES = 128
SUBLANES = 8

QKV_SWA_Q = 0
QKV_NSA_Q = 12
QKV_QUERY_CHUNKS = 24
QKV_KSL = 24
QKV_KWIN = 28
QKV_SWA_K = 32
QKV_ROPE_CHUNKS = 36
QKV_VSL = 36
QKV_VWIN = 40
QKV_SWA_V = 44
QKV_CHUNKS = 48
PROJ_TN = 512
CONV_GATE_CHUNK = 2 * CONV_CH // LANES


def _cparams(semantics, vmem_mib):
    return pltpu.CompilerParams(dimension_semantics=semantics, vmem_limit_bytes=vmem_mib * 1024 * 1024)


def _tile(n, pref):
    t = min(n, pref)
    while n % t:
        t //= 2
    return t


FFN_WBLK = 256
FFN_HIDDEN_ALIGN = 1024


def _gu_kernel(x_ref, wg0_ref, wg1_ref, wu0_ref, wu1_ref, o_ref, *, d_ff):
    j = pl.program_id(1)
    wb = wg0_ref.shape[2]
    n_real = d_ff // wb
    halves = ((wg0_ref, wu0_ref), (wg1_ref, wu1_ref))

    def step(n_live):
        x = x_ref[...]
        for half, (wg_ref, wu_ref) in enumerate(halves):
            cols = slice(half * wb, (half + 1) * wb)
            if half < n_live:
                g = jnp.dot(x, wg_ref[0], preferred_element_type=F32)
                u = jnp.dot(x, wu_ref[0], preferred_element_type=F32)
                o_ref[:, cols] = (g * jax.nn.sigmoid(g) * u).astype(o_ref.dtype)
            else:
                o_ref[:, cols] = jnp.zeros((o_ref.shape[0], wb), o_ref.dtype)

    pl.when(j < n_real // 2)(lambda: step(2))
    pl.when(j == n_real // 2)(lambda: step(n_real % 2))
    pl.when(j > n_real // 2)(lambda: step(0))


def _ffn_up(xb, w_gu, layer, d_ff):
    m, d = xb.shape
    wb = FFN_WBLK
    assert d_ff % wb == 0
    fp = -(-d_ff // FFN_HIDDEN_ALIGN) * FFN_HIDDEN_ALIGN
    up0 = d_ff // wb
    last = 2 * d_ff // wb - 1
    tm = _tile(m, 1024)
    wspec = lambda f: pl.BlockSpec((1, d, wb), lambda i, j: (layer, 0, jnp.minimum(f(j), last)))
    return pl.pallas_call(
        functools.partial(_gu_kernel, d_ff=d_ff),
        grid=(m // tm, fp // (2 * wb)),
        in_specs=[pl.BlockSpec((tm, d), lambda i, j: (i, 0)),
                  wspec(lambda j: 2 * j), wspec(lambda j: 2 * j + 1),
                  wspec(lambda j: up0 + 2 * j), wspec(lambda j: up0 + 2 * j + 1)],
        out_specs=pl.BlockSpec((tm, 2 * wb), lambda i, j: (i, j)),
        out_shape=jax.ShapeDtypeStruct((m, fp), BF16),
        compiler_params=_cparams(("parallel", "arbitrary"), 48),
        name="ffn_up",
    )(xb, w_gu, w_gu, w_gu, w_gu)


LN_ROWS = 32
LN_CHUNK = 128
MM_COLS = 1024


def _mm_res_ln_kernel(a_ref, w_ref, res_hbm, g_ref, b_ref, of_hbm, ob_hbm,
                      acc_ref, res_buf, of_buf, ob_buf, res_sem, of_sem, ob_sem, *, scale, nk, k_tail):
    i = pl.program_id(0)
    k = pl.program_id(1)
    tm, tk = a_ref.shape
    n_chunks = tm // LN_CHUNK
    row0 = i * tm

    def rows_of(c):
        return pl.ds(pl.multiple_of(row0 + c * LN_CHUNK, LN_CHUNK), LN_CHUNK)

    def res_copy(c):
        return pltpu.make_async_copy(res_hbm.at[rows_of(c), :], res_buf.at[c % 2], res_sem.at[c % 2])

    def of_copy(c):
        return pltpu.make_async_copy(of_buf.at[c % 2], of_hbm.at[rows_of(c), :], of_sem.at[c % 2])

    def ob_copy(c):
        return pltpu.make_async_copy(ob_buf.at[c % 2], ob_hbm.at[rows_of(c), :], ob_sem.at[c % 2])

    @pl.when(k == 0)
    def _():
        acc_ref[...] = jnp.zeros_like(acc_ref)

    @pl.when(k == nk - 1)
    def _():
        for c in range(min(2, n_chunks)):
            res_copy(c).start()

    def accumulate(valid_rows):
        a = a_ref[:, 0:valid_rows]
        for c in range(acc_ref.shape[1] // MM_COLS):
            cols = slice(c * MM_COLS, (c + 1) * MM_COLS)
            acc_ref[:, cols] += jnp.dot(a, w_ref[0, 0:valid_rows, cols], preferred_element_type=F32)

    if k_tail == tk:
        accumulate(tk)
    else:
        pl.when(k < nk - 1)(lambda: accumulate(tk))
        pl.when(k == nk - 1)(lambda: accumulate(k_tail))

    @pl.when(k == nk - 1)
    def _():
        gamma = g_ref[...]
        beta = b_ref[...]
        for c in range(n_chunks):
            slot = c % 2
            res_copy(c).wait()
            if c >= 2:
                of_copy(c - 2).wait()
                ob_copy(c - 2).wait()

            def rows(r, carry):
                sl = pl.ds(pl.multiple_of(r * LN_ROWS, LN_ROWS), LN_ROWS)
                src = pl.ds(pl.multiple_of(c * LN_CHUNK + r * LN_ROWS, LN_ROWS), LN_ROWS)
                y = DEEPNORM_ALPHA * res_buf[slot, sl, :] + scale * acc_ref[src, :]
                mu = jnp.mean(y, axis=-1, keepdims=True)
                dlt = y - mu
                var = jnp.mean(dlt * dlt, axis=-1, keepdims=True)
                out = dlt * lax.rsqrt(var + LN_EPS) * gamma + beta
                of_buf[slot, sl, :] = out
                ob_buf[slot, sl, :] = out.astype(ob_buf.dtype)
                return carry

            lax.fori_loop(0, LN_CHUNK // LN_ROWS, rows, 0)
            of_copy(c).start()
            ob_copy(c).start()
            if c + 2 < n_chunks:
                res_copy(c + 2).start()
        for c in range(max(n_chunks - 2, 0), n_chunks):
            of_copy(c).wait()
            ob_copy(c).wait()


def _mm_res_ln(a, w, layer, res, gamma, beta, scale):
    m, kp = a.shape
    _, k_valid, d = w.shape
    tm, tk = _tile(m, 1024), _tile(kp, 1024)
    nk = kp // tk
    assert (nk - 1) * tk < k_valid <= kp and k_valid % LANES == 0 and tm % LN_CHUNK == 0 and d % MM_COLS == 0
    hbm = pl.BlockSpec(memory_space=pl.ANY)
    return pl.pallas_call(
        functools.partial(_mm_res_ln_kernel, scale=scale, nk=nk, k_tail=k_valid - (nk - 1) * tk),
        grid=(m // tm, nk),
        in_specs=[pl.BlockSpec((tm, tk), lambda i, k: (i, k)),
                  pl.BlockSpec((1, tk, d), lambda i, k: (layer, k, 0)),
                  hbm,
                  pl.BlockSpec((1, d), lambda i, k: (0, 0)),
                  pl.BlockSpec((1, d), lambda i, k: (0, 0))],
        out_specs=[hbm, hbm],
        out_shape=[jax.ShapeDtypeStruct((m, d), F32), jax.ShapeDtypeStruct((m, d), BF16)],
        scratch_shapes=[pltpu.VMEM((tm, d), F32),
                        pltpu.VMEM((2, LN_CHUNK, d), F32), pltpu.VMEM((2, LN_CHUNK, d), F32),
                        pltpu.VMEM((2, LN_CHUNK, d), BF16),
                        pltpu.SemaphoreType.DMA((2,)), pltpu.SemaphoreType.DMA((2,)), pltpu.SemaphoreType.DMA((2,))],
        compiler_params=_cparams(("arbitrary", "arbitrary"), 56),
        name="mm_res_ln",
    )(a, w, res, gamma.reshape(1, d), beta.reshape(1, d))


def _proj_kernel(x_ref, w_ref, rope_ref, o_ref, *, n_query_tiles, n_rope_tiles):
    j = pl.program_id(1)
    acc = jnp.dot(x_ref[...], w_ref[...], preferred_element_type=F32)
    tn = acc.shape[1]

    if n_rope_tiles == 0:
        o_ref[...] = acc.astype(o_ref.dtype)
        return
    table = jnp.where(j < n_query_tiles, 1, jnp.where(j < n_rope_tiles, 0, 2))
    cc = rope_ref[table, :, 0:HEAD_DIM]
    ss = rope_ref[table, :, HEAD_DIM:2 * HEAD_DIM]
    for c in range(tn // HEAD_DIM):
        head = acc[:, c * HEAD_DIM:(c + 1) * HEAD_DIM]
        rot = head * cc + pltpu.roll(head, HEAD_DIM // 2, 1) * ss
        o_ref[:, c * HEAD_DIM:(c + 1) * HEAD_DIM] = rot.astype(o_ref.dtype)


def _proj(xb, w, rope_tables, n_query_chunks, n_rope_chunks, out_dtype):
    m, d = xb.shape
    n = w.shape[1]
    tm, tn = _tile(m, 1024), PROJ_TN
    assert n % tn == 0 and (n_rope_chunks * HEAD_DIM) % tn == 0 and (n_query_chunks * HEAD_DIM) % tn == 0
    return pl.pallas_call(
        functools.partial(_proj_kernel, n_query_tiles=n_query_chunks * HEAD_DIM // tn,
                          n_rope_tiles=n_rope_chunks * HEAD_DIM // tn),
        grid=(m // tm, n // tn),
        in_specs=[pl.BlockSpec((tm, d), lambda i, j: (i, 0)),
                  pl.BlockSpec((d, tn), lambda i, j: (0, j)),
                  pl.BlockSpec((3, tm, 2 * HEAD_DIM), lambda i, j: (0, i, 0))],
        out_specs=pl.BlockSpec((tm, tn), lambda i, j: (i, j)),
        out_shape=jax.ShapeDtypeStruct((m, n), out_dtype),
        compiler_params=_cparams(("parallel", "arbitrary"), 40),
        name="in_proj",
    )(xb, w, rope_tables)


CONV_HALO = 32
CONV_TS = 128


def _conv_kernel(v_ref, g_ref, vh_ref, gh_ref, w_ref, b_ref, lg_ref, lb_ref, o_ref, ext_ref, shift_ref):
    i = pl.program_id(1)
    ts = v_ref.shape[1]
    halo = vh_ref[0] * jax.nn.sigmoid(gh_ref[0])
    ext_ref[0:CONV_HALO, :] = jnp.where(i > 0, halo, 0.0)
    ext_ref[CONV_HALO:, :] = v_ref[0] * jax.nn.sigmoid(g_ref[0])
    rows = shift_ref.shape[1]
    for r in range(1, SUBLANES):
        shift_ref[r - 1] = ext_ref[r:r + rows, :]
    first = CONV_HALO - (CONV_WIDTH - 1)
    cols = []
    for c in range(CONV_CH // LANES):
        lanes = slice(c * LANES, (c + 1) * LANES)
        acc = jnp.zeros((ts, LANES), F32) + b_ref[:, lanes]
        for k in range(CONV_WIDTH):
            base, r = (first + k) // SUBLANES * SUBLANES, (first + k) % SUBLANES
            window = ext_ref[base:base + ts, lanes] if r == 0 else shift_ref[r - 1, base:base + ts, lanes]
            acc = acc + w_ref[k:k + 1, lanes] * window
        cols.append(acc)
    y = jnp.concatenate(cols, axis=1)
    mu = jnp.mean(y, axis=-1, keepdims=True)
    dlt = y - mu
    var = jnp.mean(dlt * dlt, axis=-1, keepdims=True)
    z = dlt * lax.rsqrt(var + LN_EPS) * lg_ref[...] + lb_ref[...]
    o_ref[0] = (z * jax.nn.sigmoid(z)).astype(o_ref.dtype)


def _conv_module(hconv, dw_w, dw_b, ln_g, ln_b):
    b, s, _ = hconv.shape
    ts = _tile(s, CONV_TS)
    per = ts // CONV_HALO
    w_pad = jnp.concatenate([dw_w, jnp.zeros((CONV_HALO - CONV_WIDTH, CONV_CH), F32)], axis=0)
    vec = lambda a: a.reshape(1, CONV_CH)
    const = lambda shape: pl.BlockSpec(shape, lambda bi, i: (0, 0))
    return pl.pallas_call(
        _conv_kernel,
        grid=(b, s // ts),
        in_specs=[pl.BlockSpec((1, ts, CONV_CH), lambda bi, i: (bi, i, 0)),
                  pl.BlockSpec((1, ts, CONV_CH), lambda bi, i: (bi, i, 1)),
                  pl.BlockSpec((1, CONV_HALO, CONV_CH), lambda bi, i: (bi, jnp.maximum(i * per - 1, 0), 0)),
                  pl.BlockSpec((1, CONV_HALO, CONV_CH), lambda bi, i: (bi, jnp.maximum(i * per - 1, 0), 1)),
                  const((CONV_HALO, CONV_CH)), const((1, CONV_CH)), const((1, CONV_CH)), const((1, CONV_CH))],
        out_specs=pl.BlockSpec((1, ts, CONV_CH), lambda bi, i: (bi, i, 0)),
        out_shape=jax.ShapeDtypeStruct((b, s, CONV_CH), BF16),
        scratch_shapes=[pltpu.VMEM((ts + CONV_HALO, CONV_CH), F32),
                        pltpu.VMEM((SUBLANES - 1, ts + CONV_HALO - SUBLANES, CONV_CH), F32)],
        compiler_params=_cparams(("parallel", "arbitrary"), 32),
        name="conv_module",
    )(hconv, hconv, hconv, hconv, w_pad, vec(dw_b), vec(ln_g), vec(ln_b))


def _compress_kernel(t_ref, pos_ref, w1_ref, w2_ref, o_ref):
    n = t_ref.shape[1] // CMP_STRIDE
    lo = jnp.zeros((n, HEAD_DIM), F32)
    hi = jnp.zeros((n, HEAD_DIM), F32)
    for l in range(CMP_STRIDE):
        tok = t_ref[0, pl.ds(l, n, stride=CMP_STRIDE), :]
        for part, first in ((0, l), (1, CMP_STRIDE + l)):
            x = (tok + pos_ref[0, first:first + 1, :]).astype(BF16)
            y = jnp.dot(x, w1_ref[0, first * HEAD_DIM:(first + 1) * HEAD_DIM, :], preferred_element_type=F32)
            lo, hi = (lo + y, hi) if part == 0 else (lo, hi + y)
    h1 = lo + pltpu.roll(hi, n - 1, 0)
    h1 = h1 * jax.nn.sigmoid(h1)
    out = jnp.dot(h1.astype(BF16), w2_ref[0], preferred_element_type=F32)
    row = lax.broadcasted_iota(jnp.int32, out.shape, 0)
    o_ref[0, 0, 0] = jnp.where(row < n - 1, out, 0.0).astype(o_ref.dtype)


def _compress(hcmp, pos, w1, w2):
    b, s, _ = hcmp.shape
    hkv = NSA_KV_HEADS
    n = s // CMP_STRIDE
    return pl.pallas_call(
        _compress_kernel,
        grid=(b, 2, hkv),
        in_specs=[pl.BlockSpec((1, s, HEAD_DIM), lambda bi, kv, h: (bi, 0, kv * hkv + h)),
                  pl.BlockSpec((1, CMP_BLOCK, HEAD_DIM), lambda bi, kv, h: (kv, 0, 0)),
                  pl.BlockSpec((1, CMP_BLOCK * HEAD_DIM, HEAD_DIM), lambda bi, kv, h: (kv, 0, 0)),
                  pl.BlockSpec((1, HEAD_DIM, HEAD_DIM), lambda bi, kv, h: (kv, 0, 0))],
        out_specs=pl.BlockSpec((1, 1, 1, n, HEAD_DIM), lambda bi, kv, h: (bi, kv, h, 0, 0)),
        out_shape=jax.ShapeDtypeStruct((b, 2, hkv, n, HEAD_DIM), BF16),
        compiler_params=_cparams(("parallel", "parallel", "arbitrary"), 40),
        name="nsa_compress",
    )(hcmp, pos, w1, w2)


_NT = (((1,), (1,)), ((), ()))


def _stack_heads(q, group):
    return jnp.concatenate([q[:, g * HEAD_DIM:(g + 1) * HEAD_DIM] for g in range(group)], axis=0)


def _online_step(qs, k, v, bias, carry):
    m, l, acc = carry
    s = lax.dot_general(qs, k, _NT, preferred_element_type=F32)
    if bias is not None:
        tq = bias.shape[0]
        s = jnp.concatenate([s[r:r + tq] + bias for r in range(0, s.shape[0], tq)], axis=0)
    m_new = jnp.maximum(m, jnp.max(s, axis=-1, keepdims=True))
    alpha = jnp.exp2(m - m_new)
    p = jnp.exp2(s - m_new)
    l_new = alpha * l + jnp.sum(p, axis=-1, keepdims=True)
    acc_new = alpha * acc + jnp.dot(p.astype(BF16), v, preferred_element_type=F32)
    return m_new, l_new, acc_new


def _online_init(rows):
    return (jnp.full((rows, 1), NEG_INF, F32), jnp.zeros((rows, 1), F32), jnp.zeros((rows, HEAD_DIM), F32))


NSA_TQ = 256
NSA_TK = 512
SEL_UNROLL = 4


def _nsa_kernel(q_ref, ksl_ref, vsl_ref, kw_ref, vw_ref, kc_ref, vc_ref, gl_ref, gb_ref, ex_ref, o_ref, *, tk_sel):
    qi = pl.program_id(2)
    tq = q_ref.shape[1]
    seq = ksl_ref.shape[1]
    n_cmp = kc_ref.shape[3]
    n_sel = seq // SEL_BLOCK
    grp = NSA_GROUP
    t0 = qi * tq
    qs = _stack_heads(q_ref[0], grp)
    tpos = t0 + lax.broadcasted_iota(jnp.int32, (tq, 1), 0)

    cmp_end = lax.broadcasted_iota(jnp.int32, (1, n_cmp), 1) * CMP_STRIDE + (CMP_BLOCK - 1)
    valid_c = cmp_end <= tpos
    s_c = lax.dot_general(qs, kc_ref[0, 0, 0], _NT, preferred_element_type=F32)
    p_parts, p_sum = [], None
    for g in range(grp):
        sm = jnp.where(valid_c, s_c[g * tq:(g + 1) * tq], NEG_INF)
        p = jnp.where(valid_c, jnp.exp2(sm - jnp.max(sm, axis=-1, keepdims=True)), 0.0)
        den = jnp.sum(p, axis=-1, keepdims=True)
        p_c = p / jnp.where(den > 0.0, den, 1.0)
        p_parts.append(p_c.astype(BF16))
        p_sum = p_c if p_sum is None else p_sum + p_c
    o_c = jnp.dot(jnp.concatenate(p_parts, axis=0), vc_ref[0, 0, 0], preferred_element_type=F32)

    n_i = lax.broadcasted_iota(jnp.int32, (n_cmp, n_sel), 0)
    j_i = lax.broadcasted_iota(jnp.int32, (n_cmp, n_sel), 1)
    ratio = SEL_BLOCK // CMP_STRIDE
    overlap = jnp.where((n_i >= ratio * j_i - (CMP_BLOCK // CMP_STRIDE - 1)) & (n_i <= ratio * j_i + ratio - 1),
                        1.0, 0.0).astype(BF16)
    p_hi = p_sum.astype(BF16)
    p_lo = (p_sum - p_hi.astype(F32)).astype(BF16)
    imp = (jnp.dot(p_hi, overlap, preferred_element_type=F32) + jnp.dot(p_lo, overlap, preferred_element_type=F32))
    blk = lax.broadcasted_iota(jnp.int32, (1, n_sel), 1)
    t_blk = jnp.right_shift(tpos, SEL_SHIFT)
    forced = (blk == 0) | (blk == t_blk) | (blk == t_blk - 1)
    imp = jnp.where(forced, FORCE_SCORE, imp)
    imp = jnp.where(blk <= t_blk, imp, NEG_INF)

    imp_t = imp.T
    rowf = lax.broadcasted_iota(jnp.int32, (n_sel, tq), 0).astype(F32)
    for _ in range(min(SEL_TOPK, n_sel)):
        mx = jnp.max(imp_t, axis=0, keepdims=True)
        idx = jnp.min(jnp.where(imp_t == mx, rowf, float(n_sel)), axis=0, keepdims=True)
        imp_t = jnp.where(rowf == idx, PICKED, imp_t)
    not_sel = jnp.where(imp_t == PICKED, 0.0, 1.0).T.astype(BF16)

    qa = jnp.concatenate([qs, jnp.concatenate([not_sel] * grp, axis=0)], axis=1)

    def sel_step(kt, carry, causal):
        k0 = pl.multiple_of(kt * tk_sel, tk_sel)
        ka = jnp.concatenate([ksl_ref[0, pl.ds(k0, tk_sel), :], ex_ref[kt]], axis=1)
        bias = None
        if causal:
            kpos = k0 + lax.broadcasted_iota(jnp.int32, (1, tk_sel), 1)
            bias = jnp.where(kpos <= tpos, 0.0, NEG_INF)
        return _online_step(qa, ka, vsl_ref[0, pl.ds(k0, tk_sel), :], bias, carry)

    n_full = t0 // tk_sel
    full_step = functools.partial(sel_step, causal=False)
    def group_step(i, carry):
        for u in range(SEL_UNROLL):
            carry = full_step(SEL_UNROLL * i + u, carry)
        return carry

    n_groups = n_full // SEL_UNROLL
    sel_out = lax.fori_loop(0, n_groups, group_step, _online_init(grp * tq))
    sel_out = lax.fori_loop(SEL_UNROLL * n_groups, n_full, full_step, sel_out)
    sel_out = sel_step(n_full, sel_out, causal=True)
    o_s = sel_out[2] / sel_out[1]

    span = NSA_WINDOW + tq
    k0 = pl.multiple_of(jnp.maximum(t0 - NSA_WINDOW, 0), tq)
    kpos = k0 + lax.broadcasted_iota(jnp.int32, (1, span), 1)
    bias = jnp.where((kpos <= tpos) & (kpos > tpos - NSA_WINDOW), 0.0, NEG_INF)
    win_out = _online_step(qs, kw_ref[0, pl.ds(k0, span), :], vw_ref[0, pl.ds(k0, span), :], bias,
                           _online_init(grp * tq))

    gate = jax.nn.sigmoid(gl_ref[0] + gb_ref[...])
    o_w = win_out[2] / win_out[1]
    for g in range(grp):
        rows = slice(g * tq, (g + 1) * tq)
        o = (gate[:, 3 * g:3 * g + 1] * o_c[rows] + gate[:, 3 * g + 1:3 * g + 2] * o_s[rows]
             + gate[:, 3 * g + 2:3 * g + 3] * o_w[rows])
        o_ref[0, :, g * HEAD_DIM:(g + 1) * HEAD_DIM] = o.astype(o_ref.dtype)


def _nsa(qkv, kvcmp, hconv, gate_b):
    b, s, _ = qkv.shape
    n = kvcmp.shape[3]
    tq = _tile(s, NSA_TQ)
    tk_sel = _tile(s, NSA_TK)
    assert NSA_WINDOW % tq == 0 and NSA_WINDOW + tq <= s and tk_sel % SEL_BLOCK == 0
    qw = NSA_GROUP * HEAD_DIM
    n_sel = s // SEL_BLOCK
    assert tk_sel % tq == 0
    key_block = np.arange(s).reshape(s // tk_sel, tk_sel, 1) // SEL_BLOCK
    onehot = key_block == np.arange(n_sel).reshape(1, 1, n_sel)
    expand = jnp.asarray(np.where(onehot, NEG_INF, 0.0), BF16)
    full = lambda chunk: pl.BlockSpec((1, s, HEAD_DIM), lambda bi, h, qi: (bi, 0, chunk + h))
    return pl.pallas_call(
        functools.partial(_nsa_kernel, tk_sel=tk_sel),
        grid=(b, NSA_KV_HEADS, s // tq),
        in_specs=[pl.BlockSpec((1, tq, qw), lambda bi, h, qi: (bi, qi, QKV_NSA_Q // NSA_GROUP + h)),
                  full(QKV_KSL), full(QKV_VSL), full(QKV_KWIN), full(QKV_VWIN),
                  pl.BlockSpec((1, 1, 1, n, HEAD_DIM), lambda bi, h, qi: (bi, 0, h, 0, 0)),
                  pl.BlockSpec((1, 1, 1, n, HEAD_DIM), lambda bi, h, qi: (bi, 1, h, 0, 0)),
                  pl.BlockSpec((1, tq, LANES), lambda bi, h, qi: (bi, qi, CONV_GATE_CHUNK + h)),
                  pl.BlockSpec((1, LANES), lambda bi, h, qi: (0, h)),
                  pl.BlockSpec(expand.shape, lambda bi, h, qi: (0, 0, 0))],
        out_specs=pl.BlockSpec((1, tq, qw), lambda bi, h, qi: (bi, qi, h)),
        out_shape=jax.ShapeDtypeStruct((b, s, NSA_HEADS * HEAD_DIM), BF16),
        compiler_params=_cparams(("parallel", "parallel", "arbitrary"), 48),
        name="nsa_attention",
    )(qkv, qkv, qkv, qkv, qkv, kvcmp, kvcmp, hconv, gate_b, expand)


SWA_TQ = 256


def _swa_kernel(sink_ref, q_ref, kp_ref, kc_ref, vp_ref, vc_ref, o_ref):
    qi = pl.program_id(1)
    tq = q_ref.shape[1]
    grp = SWA_GROUP
    t0 = qi * tq
    w = SWA_WINDOW
    for h in range(SWA_KV_HEADS):
        kv = slice(h * HEAD_DIM, (h + 1) * HEAD_DIM)
        k_all = jnp.concatenate([kp_ref[0, :, kv], kc_ref[0, :, kv]], axis=0)
        v_all = jnp.concatenate([vp_ref[0, :, kv], vc_ref[0, :, kv]], axis=0)
        sink = jnp.concatenate([jnp.full((w, 1), sink_ref[h * grp + g] * LOG2E, F32) for g in range(grp)], axis=0)
        for r in range(tq // w):
            tpos = t0 + r * w + lax.broadcasted_iota(jnp.int32, (w, 1), 0)
            kpos = t0 + (r - 1) * w + lax.broadcasted_iota(jnp.int32, (1, 2 * w), 1)
            bias = jnp.where((kpos >= 0) & (kpos <= tpos) & (kpos > tpos - w), 0.0, NEG_INF)
            qs = jnp.concatenate([q_ref[0, r * w:(r + 1) * w, (h * grp + g) * HEAD_DIM:(h * grp + g + 1) * HEAD_DIM]
                                  for g in range(grp)], axis=0)
            s = lax.dot_general(qs, k_all[r * w:(r + 2) * w], _NT, preferred_element_type=F32)
            s = jnp.concatenate([s[g * w:(g + 1) * w] + bias for g in range(grp)], axis=0)
            m = jnp.maximum(jnp.max(s, axis=-1, keepdims=True), sink)
            p = jnp.exp2(s - m)
            den = jnp.sum(p, axis=-1, keepdims=True) + jnp.exp2(sink - m)
            o = jnp.dot(p.astype(BF16), v_all[r * w:(r + 2) * w], preferred_element_type=F32) / den
            for g in range(grp):
                col = (h * grp + g) * HEAD_DIM
                o_ref[0, r * w:(r + 1) * w, col:col + HEAD_DIM] = o[g * w:(g + 1) * w].astype(o_ref.dtype)


def _swa(qkv, sinks):
    b, s, _ = qkv.shape
    tq = _tile(s, SWA_TQ)
    per = tq // SWA_WINDOW
    qw = SWA_HEADS * HEAD_DIM
    kw = SWA_KV_HEADS * HEAD_DIM
    assert QKV_SWA_Q == 0 and QKV_SWA_K % SWA_KV_HEADS == 0 and QKV_SWA_V % SWA_KV_HEADS == 0
    prev = lambda chunk: pl.BlockSpec((1, SWA_WINDOW, kw),
                                      lambda bi, qi: (bi, jnp.maximum(qi * per - 1, 0), chunk // SWA_KV_HEADS))
    cur = lambda chunk: pl.BlockSpec((1, tq, kw), lambda bi, qi: (bi, qi, chunk // SWA_KV_HEADS))
    return pl.pallas_call(
        _swa_kernel,
        grid=(b, s // tq),
        in_specs=[pl.BlockSpec(memory_space=pltpu.SMEM),
                  pl.BlockSpec((1, tq, qw), lambda bi, qi: (bi, qi, 0)),
                  prev(QKV_SWA_K), cur(QKV_SWA_K), prev(QKV_SWA_V), cur(QKV_SWA_V)],
        out_specs=pl.BlockSpec((1, tq, qw), lambda bi, qi: (bi, qi, 0)),
        out_shape=jax.ShapeDtypeStruct((b, s, qw), BF16),
        compiler_params=_cparams(("parallel", "arbitrary"), 32),
        name="swa_attention",
    )(sinks, qkv, qkv, qkv, qkv, qkv)


def _split_in_proj(w_in, gate_b):
    sizes = (CONV_CH, CONV_CH, NSA_HEADS * HEAD_DIM) + (NSA_KV_HEADS * HEAD_DIM,) * 6 + \
            (NSA_HEADS * 3, SWA_HEADS * HEAD_DIM, SWA_KV_HEADS * HEAD_DIM, SWA_KV_HEADS * HEAD_DIM)
    off = np.concatenate([[0], np.cumsum(sizes)])
    seg = lambda i: w_in[:, off[i]:off[i + 1]]
    a_val, a_gate, q_b, kc, vc, ksl, vsl, kw, vw, g_b, q_c, k_c, v_c = (seg(i) for i in range(13))
    gap = jnp.zeros((w_in.shape[0], 2 * HEAD_DIM), F32)
    w_qkv = jnp.concatenate([q_c, q_b, ksl, kw, k_c, gap, vsl, vw, v_c, gap], axis=1).astype(BF16)
    assert w_qkv.shape[1] == QKV_CHUNKS * HEAD_DIM
    w_cmp = jnp.concatenate([kc, vc], axis=1).astype(BF16)
    per_head = NSA_GROUP * 3
    pad_w = jnp.zeros((w_in.shape[0], LANES - per_head), F32)
    pad_b = jnp.zeros((LANES - per_head,), F32)
    gate_w, gate_bias = [], []
    for h in range(NSA_KV_HEADS):
        gate_w += [g_b[:, h * per_head:(h + 1) * per_head], pad_w]
        gate_bias += [gate_b[h * per_head:(h + 1) * per_head], pad_b]
    w_conv = jnp.concatenate([a_val, a_gate] + gate_w, axis=1).astype(BF16)
    return w_qkv, w_cmp, w_conv, jnp.concatenate(gate_bias).reshape(1, NSA_KV_HEADS * LANES)


def _ffn_block(x2, xb, w_gu, w_down, layer, ln_g, ln_b):
    hidden = _ffn_up(xb, w_gu, layer, w_down.shape[1])
    return _mm_res_ln(hidden, w_down, layer, x2, ln_g, ln_b, 0.5)


def _mixer_block(x2, xb, rope, batch, w_in, w_out, layer, dw_w, dw_b, cln_g, cln_b, cmp_pos, cmp_w1, cmp_w2,
                 gate_b, sinks, ln_g, ln_b):
    m = x2.shape[0]
    s = m // batch
    w_qkv, w_cmp, w_conv, gate_bias = _split_in_proj(w_in, gate_b)
    qkv = _proj(xb, w_qkv, rope, QKV_QUERY_CHUNKS, QKV_ROPE_CHUNKS, BF16).reshape(batch, s, -1)
    hcmp = _proj(xb, w_cmp, rope, 0, NSA_KV_HEADS, F32)
    hconv = _proj(xb, w_conv, rope, 0, 0, F32).reshape(batch, s, -1)
    out_a = _conv_module(hconv, dw_w, dw_b, cln_g, cln_b)
    kvcmp = _compress(hcmp.reshape(batch, s, -1), cmp_pos, cmp_w1.astype(BF16), cmp_w2.astype(BF16))
    out_b = _nsa(qkv, kvcmp, hconv, gate_bias)
    out_c = _swa(qkv, sinks)
    mixed = jnp.concatenate([out_a, out_b, out_c], axis=-1).reshape(m, -1)
    return _mm_res_ln(mixed, w_out, layer, x2, ln_g, ln_b, 1.0)


def _rope_tables(positions):
    inv = jnp.power(ROPE_THETA, -jnp.arange(0, HEAD_DIM, 2, dtype=F32) / HEAD_DIM)
    ang = positions.astype(F32).reshape(-1, 1) * inv
    cos, sin = jnp.cos(ang), jnp.sin(ang)
    table = jnp.concatenate([cos, cos, -sin, sin], axis=-1)
    keep = jnp.broadcast_to(jnp.concatenate([jnp.ones((2 * cos.shape[1],), F32), jnp.zeros((2 * cos.shape[1],), F32)]),
                            table.shape)
    return jnp.stack([table, table * QUERY_SCALE, keep])


def kernel(x, positions, ffn1_w_gu, ffn1_w_down, ln1_g, ln1_b, w_in, conv_dw_w, conv_dw_b, conv_ln_g, conv_ln_b,
           nsa_cmp_pos, nsa_cmp_w1, nsa_cmp_w2, nsa_gate_b, swa_sinks, w_out, ln2_g, ln2_b,
           ffn2_w_gu, ffn2_w_down, ln3_g, ln3_b):
    batch, s, d = x.shape
    rope = _rope_tables(positions)
    x2 = x.reshape(batch * s, d)
    xb = x2.astype(BF16)
    gu1, down1, gu2, down2, out_w = (w.astype(BF16) for w in (ffn1_w_gu, ffn1_w_down, ffn2_w_gu, ffn2_w_down, w_out))
    for i in range(w_in.shape[0]):
        x2, xb = _ffn_block(x2, xb, gu1, down1, i, ln1_g[i], ln1_b[i])
        x2, xb = _mixer_block(x2, xb, rope, batch, w_in[i], out_w, i, conv_dw_w[i], conv_dw_b[i], conv_ln_g[i],
                              conv_ln_b[i], nsa_cmp_pos[i], nsa_cmp_w1[i], nsa_cmp_w2[i], nsa_gate_b[i],
                              swa_sinks[i], ln2_g[i], ln2_b[i])
        x2, xb = _ffn_block(x2, xb, gu2, down2, i, ln3_g[i], ln3_b[i])
    return x2.reshape(batch, s, d)
```

```python
import functools

import jax
import jax.numpy as jnp
import numpy as np
from jax import lax
from jax.experimental import pallas as pl
from jax.experimental.pallas import tpu as pltpu

F32 = jnp.float32
BF16 = jnp.bfloat16

HEAD_DIM = 128
CONV_CH = 1024
CONV_WIDTH = 31
NSA_HEADS = 12
NSA_KV_HEADS = 4
NSA_GROUP = NSA_HEADS // NSA_KV_HEADS
SWA_HEADS = 12
SWA_KV_HEADS = 2
SWA_GROUP = SWA_HEADS // SWA_KV_HEADS
CMP_BLOCK = 32
CMP_STRIDE = 16
SEL_BLOCK = 64
SEL_SHIFT = 6
SEL_TOPK = 16
NSA_WINDOW = 512
SWA_WINDOW = 128
ROPE_THETA = 10000.0
LN_EPS = 1e-5
DEPTH = 2
DEEPNORM_ALPHA = (2 * DEPTH) ** 0.25
NEG_INF = -1e30
FORCE_SCORE = 1e4
PICKED = -3e38
ATTN_SCALE = HEAD_DIM ** -0.5
LOG2E = 1.4426950408889634
QUERY_SCALE = ATTN_SCALE * LOG2E

V7X_VMEM_BYTES = 64 * 1024 * 1024
LANES = 128
SUBLANES = 8

QKV_SWA_Q = 0
QKV_NSA_Q = 12
QKV_QUERY_CHUNKS = 24
QKV_KSL = 24
QKV_KWIN = 28
QKV_SWA_K = 32
QKV_ROPE_CHUNKS = 36
QKV_VSL = 36
QKV_VWIN = 40
QKV_SWA_V = 44
QKV_CHUNKS = 48
PROJ_TN = 512
CONV_GATE_CHUNK = 2 * CONV_CH // LANES


def _cparams(semantics, vmem_mib):
    assert vmem_mib * 1024 * 1024 < V7X_VMEM_BYTES
    return pltpu.CompilerParams(dimension_semantics=semantics, vmem_limit_bytes=vmem_mib * 1024 * 1024)


def _tile(n, pref):
    t = min(n, pref)
    while n % t:
        t //= 2
    return t


FFN_WBLK = 256
FFN_HIDDEN_ALIGN = 1024


def _gu_kernel(x_ref, wg0_ref, wg1_ref, wu0_ref, wu1_ref, o_ref, *, d_ff):
    j = pl.program_id(1)
    wb = wg0_ref.shape[2]
    n_real = d_ff // wb
    halves = ((wg0_ref, wu0_ref), (wg1_ref, wu1_ref))

    def step(n_live):
        x = x_ref[...]
        for half, (wg_ref, wu_ref) in enumerate(halves):
            cols = slice(half * wb, (half + 1) * wb)
            if half < n_live:
                g = jnp.dot(x, wg_ref[0], preferred_element_type=F32)
                u = jnp.dot(x, wu_ref[0], preferred_element_type=F32)
                o_ref[:, cols] = (g * jax.nn.sigmoid(g) * u).astype(o_ref.dtype)
            else:
                o_ref[:, cols] = jnp.zeros((o_ref.shape[0], wb), o_ref.dtype)

    pl.when(j < n_real // 2)(lambda: step(2))
    pl.when(j == n_real // 2)(lambda: step(n_real % 2))
    pl.when(j > n_real // 2)(lambda: step(0))


def _ffn_up(xb, w_gu, layer, d_ff):
    m, d = xb.shape
    wb = FFN_WBLK
    assert d_ff % wb == 0
    fp = -(-d_ff // FFN_HIDDEN_ALIGN) * FFN_HIDDEN_ALIGN
    up0 = d_ff // wb
    last = 2 * d_ff // wb - 1
    tm = _tile(m, 1024)
    wspec = lambda f: pl.BlockSpec((1, d, wb), lambda i, j: (layer, 0, jnp.minimum(f(j), last)))
    return pl.pallas_call(
        functools.partial(_gu_kernel, d_ff=d_ff),
        grid=(m // tm, fp // (2 * wb)),
        in_specs=[pl.BlockSpec((tm, d), lambda i, j: (i, 0)),
                  wspec(lambda j: 2 * j), wspec(lambda j: 2 * j + 1),
                  wspec(lambda j: up0 + 2 * j), wspec(lambda j: up0 + 2 * j + 1)],
        out_specs=pl.BlockSpec((tm, 2 * wb), lambda i, j: (i, j)),
        out_shape=jax.ShapeDtypeStruct((m, fp), BF16),
        compiler_params=_cparams(("parallel", "arbitrary"), 48),
        name="ffn_up",
    )(xb, w_gu, w_gu, w_gu, w_gu)


LN_ROWS = 64
LN_CHUNK = 128
MM_COLS = 1024


def _mm_res_ln_kernel(a_ref, w_ref, res_hbm, g_ref, b_ref, of_hbm, ob_hbm,
                      acc_ref, res_buf, of_buf, ob_buf, res_sem, of_sem, ob_sem, *, scale, nk, k_tail):
    i = pl.program_id(0)
    k = pl.program_id(1)
    tm, tk = a_ref.shape
    n_chunks = tm // LN_CHUNK
    row0 = i * tm

    def rows_of(c):
        return pl.ds(pl.multiple_of(row0 + c * LN_CHUNK, LN_CHUNK), LN_CHUNK)

    def res_copy(c):
        return pltpu.make_async_copy(res_hbm.at[rows_of(c), :], res_buf.at[c % 2], res_sem.at[c % 2])

    def of_copy(c):
        return pltpu.make_async_copy(of_buf.at[c % 2], of_hbm.at[rows_of(c), :], of_sem.at[c % 2])

    def ob_copy(c):
        return pltpu.make_async_copy(ob_buf.at[c % 2], ob_hbm.at[rows_of(c), :], ob_sem.at[c % 2])

    @pl.when(k == 0)
    def _():
        acc_ref[...] = jnp.zeros_like(acc_ref)

    @pl.when(k == nk - 1)
    def _():
        for c in range(min(2, n_chunks)):
            res_copy(c).start()

    def accumulate(valid_rows):
        a = a_ref[:, 0:valid_rows]
        for c in range(acc_ref.shape[1] // MM_COLS):
            cols = slice(c * MM_COLS, (c + 1) * MM_COLS)
            acc_ref[:, cols] += jnp.dot(a, w_ref[0, 0:valid_rows, cols], preferred_element_type=F32)

    if k_tail == tk:
        accumulate(tk)
    else:
        pl.when(k < nk - 1)(lambda: accumulate(tk))
        pl.when(k == nk - 1)(lambda: accumulate(k_tail))

    @pl.when(k == nk - 1)
    def _():
        gamma = g_ref[...]
        beta = b_ref[...]
        for c in range(n_chunks):
            slot = c % 2
            res_copy(c).wait()
            if c >= 2:
                of_copy(c - 2).wait()
                ob_copy(c - 2).wait()

            def rows(r, carry):
                sl = pl.ds(pl.multiple_of(r * LN_ROWS, LN_ROWS), LN_ROWS)
                src = pl.ds(pl.multiple_of(c * LN_CHUNK + r * LN_ROWS, LN_ROWS), LN_ROWS)
                y = DEEPNORM_ALPHA * res_buf[slot, sl, :] + scale * acc_ref[src, :]
                mu = jnp.mean(y, axis=-1, keepdims=True)
                dlt = y - mu
                var = jnp.mean(dlt * dlt, axis=-1, keepdims=True)
                out = dlt * lax.rsqrt(var + LN_EPS) * gamma + beta
                of_buf[slot, sl, :] = out
                ob_buf[slot, sl, :] = out.astype(ob_buf.dtype)
                return carry

            lax.fori_loop(0, LN_CHUNK // LN_ROWS, rows, 0)
            of_copy(c).start()
            ob_copy(c).start()
            if c + 2 < n_chunks:
                res_copy(c + 2).start()
        for c in range(max(n_chunks - 2, 0), n_chunks):
            of_copy(c).wait()
            ob_copy(c).wait()


def _mm_res_ln(a, w, layer, res, gamma, beta, scale):
    m, kp = a.shape
    _, k_valid, d = w.shape
    tm, tk = _tile(m, 1024), _tile(kp, 1024)
    nk = kp // tk
    assert (nk - 1) * tk < k_valid <= kp and k_valid % LANES == 0 and tm % LN_CHUNK == 0 and d % MM_COLS == 0
    hbm = pl.BlockSpec(memory_space=pl.ANY)
    return pl.pallas_call(
        functools.partial(_mm_res_ln_kernel, scale=scale, nk=nk, k_tail=k_valid - (nk - 1) * tk),
        grid=(m // tm, nk),
        in_specs=[pl.BlockSpec((tm, tk), lambda i, k: (i, k)),
                  pl.BlockSpec((1, tk, d), lambda i, k: (layer, k, 0)),
                  hbm,
                  pl.BlockSpec((1, d), lambda i, k: (0, 0)),
                  pl.BlockSpec((1, d), lambda i, k: (0, 0))],
        out_specs=[hbm, hbm],
        out_shape=[jax.ShapeDtypeStruct((m, d), F32), jax.ShapeDtypeStruct((m, d), BF16)],
        scratch_shapes=[pltpu.VMEM((tm, d), F32),
                        pltpu.VMEM((2, LN_CHUNK, d), F32), pltpu.VMEM((2, LN_CHUNK, d), F32),
                        pltpu.VMEM((2, LN_CHUNK, d), BF16),
                        pltpu.SemaphoreType.DMA((2,)), pltpu.SemaphoreType.DMA((2,)), pltpu.SemaphoreType.DMA((2,))],
        compiler_params=_cparams(("arbitrary", "arbitrary"), 56),
        name="mm_res_ln",
    )(a, w, res, gamma.reshape(1, d), beta.reshape(1, d))


def _proj_kernel(x_ref, w_ref, rope_ref, o_ref, *, n_query_tiles, n_rope_tiles):
    j = pl.program_id(1)
    acc = jnp.dot(x_ref[...], w_ref[...], preferred_element_type=F32)
    tn = acc.shape[1]

    if n_rope_tiles == 0:
        o_ref[...] = acc.astype(o_ref.dtype)
        return
    table = jnp.where(j < n_query_tiles, 1, jnp.where(j < n_rope_tiles, 0, 2))
    cc = rope_ref[table, :, 0:HEAD_DIM]
    ss = rope_ref[table, :, HEAD_DIM:2 * HEAD_DIM]
    for c in range(tn // HEAD_DIM):
        head = acc[:, c * HEAD_DIM:(c + 1) * HEAD_DIM]
        rot = head * cc + pltpu.roll(head, HEAD_DIM // 2, 1) * ss
        o_ref[:, c * HEAD_DIM:(c + 1) * HEAD_DIM] = rot.astype(o_ref.dtype)


def _proj(xb, w, rope_tables, n_query_chunks, n_rope_chunks, out_dtype):
    m, d = xb.shape
    n = w.shape[1]
    tm, tn = _tile(m, 1024), PROJ_TN
    assert n % tn == 0 and (n_rope_chunks * HEAD_DIM) % tn == 0 and (n_query_chunks * HEAD_DIM) % tn == 0
    return pl.pallas_call(
        functools.partial(_proj_kernel, n_query_tiles=n_query_chunks * HEAD_DIM // tn,
                          n_rope_tiles=n_rope_chunks * HEAD_DIM // tn),
        grid=(m // tm, n // tn),
        in_specs=[pl.BlockSpec((tm, d), lambda i, j: (i, 0)),
                  pl.BlockSpec((d, tn), lambda i, j: (0, j)),
                  pl.BlockSpec((3, tm, 2 * HEAD_DIM), lambda i, j: (0, i, 0))],
        out_specs=pl.BlockSpec((tm, tn), lambda i, j: (i, j)),
        out_shape=jax.ShapeDtypeStruct((m, n), out_dtype),
        compiler_params=_cparams(("parallel", "arbitrary"), 40),
        name="in_proj",
    )(xb, w, rope_tables)


CONV_HALO = 32
CONV_TS = 128


def _conv_kernel(v_ref, g_ref, vh_ref, gh_ref, w_ref, b_ref, lg_ref, lb_ref, o_ref, ext_ref, shift_ref):
    i = pl.program_id(1)
    ts = v_ref.shape[1]
    halo = vh_ref[0] * jax.nn.sigmoid(gh_ref[0])
    ext_ref[0:CONV_HALO, :] = jnp.where(i > 0, halo, 0.0)
    ext_ref[CONV_HALO:, :] = v_ref[0] * jax.nn.sigmoid(g_ref[0])
    rows = shift_ref.shape[1]
    for r in range(1, SUBLANES):
        shift_ref[r - 1] = ext_ref[r:r + rows, :]
    first = CONV_HALO - (CONV_WIDTH - 1)
    cols = []
    for c in range(CONV_CH // LANES):
        lanes = slice(c * LANES, (c + 1) * LANES)
        acc = jnp.zeros((ts, LANES), F32) + b_ref[:, lanes]
        for k in range(CONV_WIDTH):
            base, r = (first + k) // SUBLANES * SUBLANES, (first + k) % SUBLANES
            window = ext_ref[base:base + ts, lanes] if r == 0 else shift_ref[r - 1, base:base + ts, lanes]
            acc = acc + w_ref[k:k + 1, lanes] * window
        cols.append(acc)
    y = jnp.concatenate(cols, axis=1)
    mu = jnp.mean(y, axis=-1, keepdims=True)
    dlt = y - mu
    var = jnp.mean(dlt * dlt, axis=-1, keepdims=True)
    z = dlt * lax.rsqrt(var + LN_EPS) * lg_ref[...] + lb_ref[...]
    o_ref[0] = (z * jax.nn.sigmoid(z)).astype(o_ref.dtype)


def _conv_module(hconv, dw_w, dw_b, ln_g, ln_b):
    b, s, _ = hconv.shape
    ts = _tile(s, CONV_TS)
    per = ts // CONV_HALO
    w_pad = jnp.concatenate([dw_w, jnp.zeros((CONV_HALO - CONV_WIDTH, CONV_CH), F32)], axis=0)
    vec = lambda a: a.reshape(1, CONV_CH)
    const = lambda shape: pl.BlockSpec(shape, lambda bi, i: (0, 0))
    return pl.pallas_call(
        _conv_kernel,
        grid=(b, s // ts),
        in_specs=[pl.BlockSpec((1, ts, CONV_CH), lambda bi, i: (bi, i, 0)),
                  pl.BlockSpec((1, ts, CONV_CH), lambda bi, i: (bi, i, 1)),
                  pl.BlockSpec((1, CONV_HALO, CONV_CH), lambda bi, i: (bi, jnp.maximum(i * per - 1, 0), 0)),
                  pl.BlockSpec((1, CONV_HALO, CONV_CH), lambda bi, i: (bi, jnp.maximum(i * per - 1, 0), 1)),
                  const((CONV_HALO, CONV_CH)), const((1, CONV_CH)), const((1, CONV_CH)), const((1, CONV_CH))],
        out_specs=pl.BlockSpec((1, ts, CONV_CH), lambda bi, i: (bi, i, 0)),
        out_shape=jax.ShapeDtypeStruct((b, s, CONV_CH), BF16),
        scratch_shapes=[pltpu.VMEM((ts + CONV_HALO, CONV_CH), F32),
                        pltpu.VMEM((SUBLANES - 1, ts + CONV_HALO - SUBLANES, CONV_CH), F32)],
        compiler_params=_cparams(("parallel", "arbitrary"), 32),
        name="conv_module",
    )(hconv, hconv, hconv, hconv, w_pad, vec(dw_b), vec(ln_g), vec(ln_b))


def _compress_kernel(t_ref, pos_ref, w1_ref, w2_ref, o_ref):
    n = t_ref.shape[1] // CMP_STRIDE
    lo = jnp.zeros((n, HEAD_DIM), F32)
    hi = jnp.zeros((n, HEAD_DIM), F32)
    for l in range(CMP_STRIDE):
        tok = t_ref[0, pl.ds(l, n, stride=CMP_STRIDE), :]
        for part, first in ((0, l), (1, CMP_STRIDE + l)):
            x = (tok + pos_ref[0, first:first + 1, :]).astype(BF16)
            y = jnp.dot(x, w1_ref[0, first * HEAD_DIM:(first + 1) * HEAD_DIM, :], preferred_element_type=F32)
            lo, hi = (lo + y, hi) if part == 0 else (lo, hi + y)
    h1 = lo + pltpu.roll(hi, n - 1, 0)
    h1 = h1 * jax.nn.sigmoid(h1)
    out = jnp.dot(h1.astype(BF16), w2_ref[0], preferred_element_type=F32)
    row = lax.broadcasted_iota(jnp.int32, out.shape, 0)
    o_ref[0, 0, 0] = jnp.where(row < n - 1, out, 0.0).astype(o_ref.dtype)


def _compress(hcmp, pos, w1, w2):
    b, s, _ = hcmp.shape
    hkv = NSA_KV_HEADS
    n = s // CMP_STRIDE
    return pl.pallas_call(
        _compress_kernel,
        grid=(b, 2, hkv),
        in_specs=[pl.BlockSpec((1, s, HEAD_DIM), lambda bi, kv, h: (bi, 0, kv * hkv + h)),
                  pl.BlockSpec((1, CMP_BLOCK, HEAD_DIM), lambda bi, kv, h: (kv, 0, 0)),
                  pl.BlockSpec((1, CMP_BLOCK * HEAD_DIM, HEAD_DIM), lambda bi, kv, h: (kv, 0, 0)),
                  pl.BlockSpec((1, HEAD_DIM, HEAD_DIM), lambda bi, kv, h: (kv, 0, 0))],
        out_specs=pl.BlockSpec((1, 1, 1, n, HEAD_DIM), lambda bi, kv, h: (bi, kv, h, 0, 0)),
        out_shape=jax.ShapeDtypeStruct((b, 2, hkv, n, HEAD_DIM), BF16),
        compiler_params=_cparams(("parallel", "parallel", "arbitrary"), 40),
        name="nsa_compress",
    )(hcmp, pos, w1, w2)


_NT = (((1,), (1,)), ((), ()))


def _stack_heads(q, group):
    return jnp.concatenate([q[:, g * HEAD_DIM:(g + 1) * HEAD_DIM] for g in range(group)], axis=0)


def _online_step(qs, k, v, bias, carry):
    m, l, acc = carry
    s = lax.dot_general(qs, k, _NT, preferred_element_type=F32)
    if bias is not None:
        tq = bias.shape[0]
        s = jnp.concatenate([s[r:r + tq] + bias for r in range(0, s.shape[0], tq)], axis=0)
    m_new = jnp.maximum(m, jnp.max(s, axis=-1, keepdims=True))
    alpha = jnp.exp2(m - m_new)
    p = jnp.exp2(s - m_new)
    l_new = alpha * l + jnp.sum(p, axis=-1, keepdims=True)
    acc_new = alpha * acc + jnp.dot(p.astype(BF16), v, preferred_element_type=F32)
    return m_new, l_new, acc_new


def _online_init(rows):
    return (jnp.full((rows, 1), NEG_INF, F32), jnp.zeros((rows, 1), F32), jnp.zeros((rows, HEAD_DIM), F32))


NSA_TQ = 256
NSA_TK = 512
SEL_UNROLL = 4


def _nsa_kernel(q_ref, ksl_ref, vsl_ref, kw_ref, vw_ref, kc_ref, vc_ref, gl_ref, gb_ref, ex_ref, o_ref, *, tk_sel):
    qi = pl.program_id(2)
    tq = q_ref.shape[1]
    seq = ksl_ref.shape[1]
    n_cmp = kc_ref.shape[3]
    n_sel = seq // SEL_BLOCK
    grp = NSA_GROUP
    t0 = qi * tq
    qs = _stack_heads(q_ref[0], grp)
    tpos = t0 + lax.broadcasted_iota(jnp.int32, (tq, 1), 0)

    cmp_end = lax.broadcasted_iota(jnp.int32, (1, n_cmp), 1) * CMP_STRIDE + (CMP_BLOCK - 1)
    valid_c = cmp_end <= tpos
    s_c = lax.dot_general(qs, kc_ref[0, 0, 0], _NT, preferred_element_type=F32)
    p_parts, p_sum = [], None
    for g in range(grp):
        sm = jnp.where(valid_c, s_c[g * tq:(g + 1) * tq], NEG_INF)
        p = jnp.where(valid_c, jnp.exp2(sm - jnp.max(sm, axis=-1, keepdims=True)), 0.0)
        den = jnp.sum(p, axis=-1, keepdims=True)
        p_c = p / jnp.where(den > 0.0, den, 1.0)
        p_parts.append(p_c.astype(BF16))
        p_sum = p_c if p_sum is None else p_sum + p_c
    o_c = jnp.dot(jnp.concatenate(p_parts, axis=0), vc_ref[0, 0, 0], preferred_element_type=F32)

    n_i = lax.broadcasted_iota(jnp.int32, (n_cmp, n_sel), 0)
    j_i = lax.broadcasted_iota(jnp.int32, (n_cmp, n_sel), 1)
    ratio = SEL_BLOCK // CMP_STRIDE
    overlap = jnp.where((n_i >= ratio * j_i - (CMP_BLOCK // CMP_STRIDE - 1)) & (n_i <= ratio * j_i + ratio - 1),
                        1.0, 0.0).astype(BF16)
    p_hi = p_sum.astype(BF16)
    p_lo = (p_sum - p_hi.astype(F32)).astype(BF16)
    imp = (jnp.dot(p_hi, overlap, preferred_element_type=F32) + jnp.dot(p_lo, overlap, preferred_element_type=F32))
    blk = lax.broadcasted_iota(jnp.int32, (1, n_sel), 1)
    t_blk = jnp.right_shift(tpos, SEL_SHIFT)
    forced = (blk == 0) | (blk == t_blk) | (blk == t_blk - 1)
    imp = jnp.where(forced, FORCE_SCORE, imp)
    imp = jnp.where(blk <= t_blk, imp, NEG_INF)

    imp_t = imp.T
    rowf = lax.broadcasted_iota(jnp.int32, (n_sel, tq), 0).astype(F32)
    for _ in range(min(SEL_TOPK, n_sel)):
        mx = jnp.max(imp_t, axis=0, keepdims=True)
        idx = jnp.min(jnp.where(imp_t == mx, rowf, float(n_sel)), axis=0, keepdims=True)
        imp_t = jnp.where(rowf == idx, PICKED, imp_t)
    not_sel = jnp.where(imp_t == PICKED, 0.0, 1.0).T.astype(BF16)

    qa = jnp.concatenate([qs, jnp.concatenate([not_sel] * grp, axis=0)], axis=1)

    def sel_step(kt, carry, causal):
        k0 = pl.multiple_of(kt * tk_sel, tk_sel)
        ka = jnp.concatenate([ksl_ref[0, pl.ds(k0, tk_sel), :], ex_ref[kt]], axis=1)
        bias = None
        if causal:
            kpos = k0 + lax.broadcasted_iota(jnp.int32, (1, tk_sel), 1)
            bias = jnp.where(kpos <= tpos, 0.0, NEG_INF)
        return _online_step(qa, ka, vsl_ref[0, pl.ds(k0, tk_sel), :], bias, carry)

    n_full = t0 // tk_sel
    full_step = functools.partial(sel_step, causal=False)
    def group_step(i, carry):
        for u in range(SEL_UNROLL):
            carry = full_step(SEL_UNROLL * i + u, carry)
        return carry

    n_groups = n_full // SEL_UNROLL
    sel_out = lax.fori_loop(0, n_groups, group_step, _online_init(grp * tq))
    sel_out = lax.fori_loop(SEL_UNROLL * n_groups, n_full, full_step, sel_out)
    sel_out = sel_step(n_full, sel_out, causal=True)
    o_s = sel_out[2] / sel_out[1]

    span = NSA_WINDOW + tq
    k0 = pl.multiple_of(jnp.maximum(t0 - NSA_WINDOW, 0), tq)
    kpos = k0 + lax.broadcasted_iota(jnp.int32, (1, span), 1)
    bias = jnp.where((kpos <= tpos) & (kpos > tpos - NSA_WINDOW), 0.0, NEG_INF)
    win_out = _online_step(qs, kw_ref[0, pl.ds(k0, span), :], vw_ref[0, pl.ds(k0, span), :], bias,
                           _online_init(grp * tq))

    gate = jax.nn.sigmoid(gl_ref[0] + gb_ref[...])
    o_w = win_out[2] / win_out[1]
    for g in range(grp):
        rows = slice(g * tq, (g + 1) * tq)
        o = (gate[:, 3 * g:3 * g + 1] * o_c[rows] + gate[:, 3 * g + 1:3 * g + 2] * o_s[rows]
             + gate[:, 3 * g + 2:3 * g + 3] * o_w[rows])
        o_ref[0, :, g * HEAD_DIM:(g + 1) * HEAD_DIM] = o.astype(o_ref.dtype)


def _nsa(qkv, kvcmp, hconv, gate_b):
    b, s, _ = qkv.shape
    n = kvcmp.shape[3]
    tq = _tile(s, NSA_TQ)
    tk_sel = _tile(s, NSA_TK)
    assert NSA_WINDOW % tq == 0 and NSA_WINDOW + tq <= s and tk_sel % SEL_BLOCK == 0
    qw = NSA_GROUP * HEAD_DIM
    n_sel = s // SEL_BLOCK
    assert tk_sel % tq == 0
    key_block = np.arange(s).reshape(s // tk_sel, tk_sel, 1) // SEL_BLOCK
    onehot = key_block == np.arange(n_sel).reshape(1, 1, n_sel)
    expand = jnp.asarray(np.where(onehot, NEG_INF, 0.0), BF16)
    full = lambda chunk: pl.BlockSpec((1, s, HEAD_DIM), lambda bi, h, qi: (bi, 0, chunk + h))
    return pl.pallas_call(
        functools.partial(_nsa_kernel, tk_sel=tk_sel),
        grid=(b, NSA_KV_HEADS, s // tq),
        in_specs=[pl.BlockSpec((1, tq, qw), lambda bi, h, qi: (bi, qi, QKV_NSA_Q // NSA_GROUP + h)),
                  full(QKV_KSL), full(QKV_VSL), full(QKV_KWIN), full(QKV_VWIN),
                  pl.BlockSpec((1, 1, 1, n, HEAD_DIM), lambda bi, h, qi: (bi, 0, h, 0, 0)),
                  pl.BlockSpec((1, 1, 1, n, HEAD_DIM), lambda bi, h, qi: (bi, 1, h, 0, 0)),
                  pl.BlockSpec((1, tq, LANES), lambda bi, h, qi: (bi, qi, CONV_GATE_CHUNK + h)),
                  pl.BlockSpec((1, LANES), lambda bi, h, qi: (0, h)),
                  pl.BlockSpec(expand.shape, lambda bi, h, qi: (0, 0, 0))],
        out_specs=pl.BlockSpec((1, tq, qw), lambda bi, h, qi: (bi, qi, h)),
        out_shape=jax.ShapeDtypeStruct((b, s, NSA_HEADS * HEAD_DIM), BF16),
        compiler_params=_cparams(("parallel", "parallel", "arbitrary"), 48),
        name="nsa_attention",
    )(qkv, qkv, qkv, qkv, qkv, kvcmp, kvcmp, hconv, gate_b, expand)


SWA_TQ = 1024


def _swa_kernel(sink_ref, q_ref, kp_ref, kc_ref, vp_ref, vc_ref, o_ref):
    qi = pl.program_id(1)
    tq = q_ref.shape[1]
    grp = SWA_GROUP
    t0 = qi * tq
    w = SWA_WINDOW
    for h in range(SWA_KV_HEADS):
        kv = slice(h * HEAD_DIM, (h + 1) * HEAD_DIM)
        k_all = jnp.concatenate([kp_ref[0, :, kv], kc_ref[0, :, kv]], axis=0)
        v_all = jnp.concatenate([vp_ref[0, :, kv], vc_ref[0, :, kv]], axis=0)
        sink = jnp.concatenate([jnp.full((w, 1), sink_ref[h * grp + g] * LOG2E, F32) for g in range(grp)], axis=0)
        for r in range(tq // w):
            tpos = t0 + r * w + lax.broadcasted_iota(jnp.int32, (w, 1), 0)
            kpos = t0 + (r - 1) * w + lax.broadcasted_iota(jnp.int32, (1, 2 * w), 1)
            bias = jnp.where((kpos >= 0) & (kpos <= tpos) & (kpos > tpos - w), 0.0, NEG_INF)
            qs = jnp.concatenate([q_ref[0, r * w:(r + 1) * w, (h * grp + g) * HEAD_DIM:(h * grp + g + 1) * HEAD_DIM]
                                  for g in range(grp)], axis=0)
            s = lax.dot_general(qs, k_all[r * w:(r + 2) * w], _NT, preferred_element_type=F32)
            s = jnp.concatenate([s[g * w:(g + 1) * w] + bias for g in range(grp)], axis=0)
            m = jnp.maximum(jnp.max(s, axis=-1, keepdims=True), sink)
            p = jnp.exp2(s - m)
            den = jnp.sum(p, axis=-1, keepdims=True) + jnp.exp2(sink - m)
            o = jnp.dot(p.astype(BF16), v_all[r * w:(r + 2) * w], preferred_element_type=F32) / den
            for g in range(grp):
                col = (h * grp + g) * HEAD_DIM
                o_ref[0, r * w:(r + 1) * w, col:col + HEAD_DIM] = o[g * w:(g + 1) * w].astype(o_ref.dtype)


def _swa(qkv, sinks):
    b, s, _ = qkv.shape
    tq = _tile(s, SWA_TQ)
    per = tq // SWA_WINDOW
    qw = SWA_HEADS * HEAD_DIM
    kw = SWA_KV_HEADS * HEAD_DIM
    assert QKV_SWA_Q == 0 and QKV_SWA_K % SWA_KV_HEADS == 0 and QKV_SWA_V % SWA_KV_HEADS == 0
    prev = lambda chunk: pl.BlockSpec((1, SWA_WINDOW, kw),
                                      lambda bi, qi: (bi, jnp.maximum(qi * per - 1, 0), chunk // SWA_KV_HEADS))
    cur = lambda chunk: pl.BlockSpec((1, tq, kw), lambda bi, qi: (bi, qi, chunk // SWA_KV_HEADS))
    return pl.pallas_call(
        _swa_kernel,
        grid=(b, s // tq),
        in_specs=[pl.BlockSpec(memory_space=pltpu.SMEM),
                  pl.BlockSpec((1, tq, qw), lambda bi, qi: (bi, qi, 0)),
                  prev(QKV_SWA_K), cur(QKV_SWA_K), prev(QKV_SWA_V), cur(QKV_SWA_V)],
        out_specs=pl.BlockSpec((1, tq, qw), lambda bi, qi: (bi, qi, 0)),
        out_shape=jax.ShapeDtypeStruct((b, s, qw), BF16),
        compiler_params=_cparams(("parallel", "arbitrary"), 32),
        name="swa_attention",
    )(sinks, qkv, qkv, qkv, qkv, qkv)


def _split_in_proj(w_in, gate_b):
    sizes = (CONV_CH, CONV_CH, NSA_HEADS * HEAD_DIM) + (NSA_KV_HEADS * HEAD_DIM,) * 6 + \
            (NSA_HEADS * 3, SWA_HEADS * HEAD_DIM, SWA_KV_HEADS * HEAD_DIM, SWA_KV_HEADS * HEAD_DIM)
    off = np.concatenate([[0], np.cumsum(sizes)])
    seg = lambda i: w_in[:, off[i]:off[i + 1]]
    a_val, a_gate, q_b, kc, vc, ksl, vsl, kw, vw, g_b, q_c, k_c, v_c = (seg(i) for i in range(13))
    gap = jnp.zeros((w_in.shape[0], 2 * HEAD_DIM), F32)
    w_qkv = jnp.concatenate([q_c, q_b, ksl, kw, k_c, gap, vsl, vw, v_c, gap], axis=1).astype(BF16)
    assert w_qkv.shape[1] == QKV_CHUNKS * HEAD_DIM
    w_cmp = jnp.concatenate([kc, vc], axis=1).astype(BF16)
    per_head = NSA_GROUP * 3
    pad_w = jnp.zeros((w_in.shape[0], LANES - per_head), F32)
    pad_b = jnp.zeros((LANES - per_head,), F32)
    gate_w, gate_bias = [], []
    for h in range(NSA_KV_HEADS):
        gate_w += [g_b[:, h * per_head:(h + 1) * per_head], pad_w]
        gate_bias += [gate_b[h * per_head:(h + 1) * per_head], pad_b]
    w_conv = jnp.concatenate([a_val, a_gate] + gate_w, axis=1).astype(BF16)
    return w_qkv, w_cmp, w_conv, jnp.concatenate(gate_bias).reshape(1, NSA_KV_HEADS * LANES)


def _ffn_block(x2, xb, w_gu, w_down, layer, ln_g, ln_b):
    hidden = _ffn_up(xb, w_gu, layer, w_down.shape[1])
    return _mm_res_ln(hidden, w_down, layer, x2, ln_g, ln_b, 0.5)


def _mixer_block(x2, xb, rope, batch, w_in, w_out, layer, dw_w, dw_b, cln_g, cln_b, cmp_pos, cmp_w1, cmp_w2,
                 gate_b, sinks, ln_g, ln_b):
    m = x2.shape[0]
    s = m // batch
    w_qkv, w_cmp, w_conv, gate_bias = _split_in_proj(w_in, gate_b)
    qkv = _proj(xb, w_qkv, rope, QKV_QUERY_CHUNKS, QKV_ROPE_CHUNKS, BF16).reshape(batch, s, -1)
    hcmp = _proj(xb, w_cmp, rope, 0, NSA_KV_HEADS, F32)
    hconv = _proj(xb, w_conv, rope, 0, 0, F32).reshape(batch, s, -1)
    out_a = _conv_module(hconv, dw_w, dw_b, cln_g, cln_b)
    kvcmp = _compress(hcmp.reshape(batch, s, -1), cmp_pos, cmp_w1.astype(BF16), cmp_w2.astype(BF16))
    out_b = _nsa(qkv, kvcmp, hconv, gate_bias)
    out_c = _swa(qkv, sinks)
    mixed = jnp.concatenate([out_a, out_b, out_c], axis=-1).reshape(m, -1)
    return _mm_res_ln(mixed, w_out, layer, x2, ln_g, ln_b, 1.0)


def _rope_tables(positions):
    inv = jnp.power(ROPE_THETA, -jnp.arange(0, HEAD_DIM, 2, dtype=F32) / HEAD_DIM)
    ang = positions.astype(F32).reshape(-1, 1) * inv
    cos, sin = jnp.cos(ang), jnp.sin(ang)
    table = jnp.concatenate([cos, cos, -sin, sin], axis=-1)
    keep = jnp.broadcast_to(jnp.concatenate([jnp.ones((2 * cos.shape[1],), F32), jnp.zeros((2 * cos.shape[1],), F32)]),
                            table.shape)
    return jnp.stack([table, table * QUERY_SCALE, keep])


def kernel(x, positions, ffn1_w_gu, ffn1_w_down, ln1_g, ln1_b, w_in, conv_dw_w, conv_dw_b, conv_ln_g, conv_ln_b,
           nsa_cmp_pos, nsa_cmp_w1, nsa_cmp_w2, nsa_gate_b, swa_sinks, w_out, ln2_g, ln2_b,
           ffn2_w_gu, ffn2_w_down, ln3_g, ln3_b):
    batch, s, d = x.shape
    rope = _rope_tables(positions)
    x2 = x.reshape(batch * s, d)
    xb = x2.astype(BF16)
    gu1, down1, gu2, down2, out_w = (w.astype(BF16) for w in (ffn1_w_gu, ffn1_w_down, ffn2_w_gu, ffn2_w_down, w_out))
    for i in range(w_in.shape[0]):
        x2, xb = _ffn_block(x2, xb, gu1, down1, i, ln1_g[i], ln1_b[i])
        x2, xb = _mixer_block(x2, xb, rope, batch, w_in[i], out_w, i, conv_dw_w[i], conv_dw_b[i], conv_ln_g[i],
                              conv_ln_b[i], nsa_cmp_pos[i], nsa_cmp_w1[i], nsa_cmp_w2[i], nsa_gate_b[i],
                              swa_sinks[i], ln2_g[i], ln2_b[i])
        x2, xb = _ffn_block(x2, xb, gu2, down2, i, ln3_g[i], ln3_b[i])
    return x2.reshape(batch, s, d)
```
